```python
import jax
import jax.numpy as jnp
from jax import lax
import numpy as np

D_MODEL = 1024
BATCH = 4
SEQ = 4096
DEPTH = 2

POOL_WINDOWS = (2, 4, 8, 16)
POOL_GROUPS = 4
POOL_GROUP_W = D_MODEL // 16
POOL_W = POOL_GROUPS * POOL_GROUP_W
CONV_W = D_MODEL // 4
CONV_K = 31
RWKV_HEAD = 64
RWKV_W = D_MODEL // 2
RWKV_HEADS = RWKV_W // RWKV_HEAD
W_LORA = 32
A_LORA = 32
V_LORA = 16
G_LORA = 64
RWKV_COLS = 3 * RWKV_W + W_LORA + A_LORA + G_LORA
RWKV_SPLITS = (RWKV_W, 2 * RWKV_W, 3 * RWKV_W, 3 * RWKV_W + W_LORA, 3 * RWKV_W + W_LORA + A_LORA)
N_BRANCH = 3
OFF_CONV = POOL_W
OFF_RWKV = OFF_CONV + 2 * CONV_W
OFF_GATE = OFF_RWKV + RWKV_COLS
OFF_VRES = OFF_GATE + N_BRANCH * D_MODEL
IN_COLS = OFF_VRES
D_FF = 2816
N_EXPERTS = 8
TOP_K = 2
D_FF_EXPERT = 3584
N_DENSE = (DEPTH + 1) // 2
N_MOE = DEPTH // 2
RMS_EPS = 1e-6
LN_EPS = 1e-5
GN_EPS = 64e-5

kernel_name = 'hybrid_pool_conv_rwkv7_moe_adaln'


def _rms(x, g):
    xf = x.astype(jnp.float32)
    y = xf * lax.rsqrt(jnp.mean(xf * xf, axis=-1, keepdims=True) + RMS_EPS)
    return (y * g.astype(jnp.float32)).astype(x.dtype)


def _ada(x, c, w, b, g):
    mod = jax.nn.silu(c) @ w + b
    shift, scale, gate = jnp.split(mod, 3, axis=-1)
    h = _rms(x, g) * (1 + scale[:, None, :]) + shift[:, None, :]
    return h, gate[:, None, :]


def _shift(z):
    return jnp.pad(z, ((0, 0), (1, 0), (0, 0)))[:, :-1]


def _pool_mixer(u, pool_w, pool_scale, pool_proj):
    b_, s_, _ = u.shape
    uf = u.astype(jnp.float32).reshape(b_, s_, POOL_GROUPS, POOL_GROUP_W)
    cs = jnp.cumsum(uf, axis=1)
    pos = jnp.arange(1, s_ + 1, dtype=jnp.float32)
    outs = []
    for gi, win in enumerate(POOL_WINDOWS):
        csg = cs[:, :, gi]
        lag = jnp.pad(csg, ((0, 0), (win, 0), (0, 0)))[:, :s_]
        cnt = jnp.minimum(pos, float(win))[None, :, None]
        outs.append((csg - lag) / cnt - uf[:, :, gi])
    p = jnp.stack(outs, axis=2).astype(u.dtype)
    p = jnp.einsum('bsgc,gcd->bsgd', p, pool_w).reshape(b_, s_, POOL_W) * pool_scale
    return p @ pool_proj


def _conv_module(z, conv_w, conv_b, ln_g, ln_b, proj):
    a, b = jnp.split(z, 2, axis=-1)
    u = a * jax.nn.sigmoid(b)
    u = lax.conv_general_dilated(u, conv_w[:, None, :].astype(u.dtype), (1,), ((CONV_K - 1, 0),),
                                 dimension_numbers=('NWC', 'WIO', 'NWC'),
                                 feature_group_count=CONV_W) + conv_b
    uf = u.astype(jnp.float32)
    mean = jnp.mean(uf, axis=-1, keepdims=True)
    var = jnp.mean(jnp.square(uf - mean), axis=-1, keepdims=True)
    un = ((uf - mean) * lax.rsqrt(var + LN_EPS)).astype(u.dtype) * ln_g + ln_b
    return jax.nn.silu(un) @ proj


def _wkv7_scan(r, w, k, v, kk, a):
    def step(state, inp):
        r_t, w_t, k_t, v_t, kk_t, a_t = inp
        sa = jnp.einsum('bhij,bhj->bhi', state, -kk_t)
        state = (state * w_t[:, :, None, :]
                 + sa[..., None] * (kk_t * a_t)[:, :, None, :]
                 + v_t[..., None] * k_t[:, :, None, :])
        return state, jnp.einsum('bhij,bhj->bhi', state, r_t)
    b_, s_, h_, n_ = r.shape
    s0 = jnp.zeros((b_, h_, n_, n_), jnp.float32)
    xs = tuple(jnp.moveaxis(t, 1, 0) for t in (r, w, k, v, kk, a))
    _, out = lax.scan(step, s0, xs)
    return jnp.moveaxis(out, 0, 1)


def _rwkv7_branch(zr, mu, w0, w2, a0, a2, g2, k_scale, k_a, r_k, lnx_g, lnx_b, proj, vres):
    f32 = jnp.float32
    b_, s_, _ = zr.shape
    hd = lambda t: t.astype(f32).reshape(b_, s_, RWKV_HEADS, RWKV_HEAD)
    zr = zr + (_shift(zr) - zr) * mu
    r, k, v, wd, ad, gd = jnp.split(zr, RWKV_SPLITS, axis=-1)
    w = -jax.nn.softplus(-(w0 + jnp.tanh(wd) @ w2).astype(f32)) - 0.5
    decay = jnp.exp(-jnp.exp(w))
    a = jax.nn.sigmoid((a0 + ad @ a2).astype(f32))
    g = jax.nn.sigmoid(gd) @ g2
    if vres is not None:
        v_first, vd, v0, v2 = vres
        v = v + (v_first - v) * jax.nn.sigmoid(v0 + vd @ v2)
    kk = hd(k * k_scale)
    kk = kk / jnp.maximum(jnp.sqrt(jnp.sum(kk * kk, axis=-1, keepdims=True)), 1e-12)
    kh = hd(k * (1 + (a - 1) * k_a))
    rh = hd(r)
    vh = hd(v)
    o = _wkv7_scan(rh, hd(decay), kh, vh, kk, hd(a))
    mean = jnp.mean(o, axis=-1, keepdims=True)
    var = jnp.mean(jnp.square(o - mean), axis=-1, keepdims=True)
    o = ((o - mean) * lax.rsqrt(var + GN_EPS)).reshape(b_, s_, RWKV_W) * lnx_g + lnx_b
    bonus = jnp.sum(rh * kh * r_k.reshape(RWKV_HEADS, RWKV_HEAD), axis=-1, keepdims=True) * vh
    o = (o + bonus.reshape(b_, s_, RWKV_W)) * g
    return o.astype(zr.dtype) @ proj, v


def _swiglu(h, w1, w3, w2):
    return (jax.nn.silu(h @ w1) * (h @ w3)) @ w2


def _moe(h, router, w1, w3, w2):
    b_, s_, d_ = h.shape
    t = h.reshape(-1, d_)
    logits = (t @ router).astype(jnp.float32)
    vals, idx = lax.top_k(logits, TOP_K)
    wts = jax.nn.softmax(vals, axis=-1)
    gate = jnp.sum(jax.nn.one_hot(idx, N_EXPERTS, dtype=jnp.float32) * wts[..., None], axis=1).astype(h.dtype)
    y = jnp.zeros_like(t)
    for e in range(N_EXPERTS):
        y = y + gate[:, e:e + 1] * _swiglu(t, w1[e], w3[e], w2[e])
    return y.reshape(b_, s_, d_)


def setup_inputs(seed: int = 0) -> dict:
    key = jax.random.key(seed)
    ks = iter(jax.random.split(key, 64))
    nrm = lambda shape, s: jax.random.normal(next(ks), shape, jnp.float32) * s
    uni = lambda shape: jax.random.uniform(next(ks), shape, jnp.float32)
    D = D_MODEL
    L = DEPTH
    LV = DEPTH - 1
    w0 = jnp.broadcast_to(jnp.linspace(-6.0, 1.0, RWKV_W, dtype=jnp.float32), (L, RWKV_W)) + nrm((L, RWKV_W), 0.1)
    return {
        'x': nrm((BATCH, SEQ, D), 1.0),
        'c': nrm((BATCH, D), 1.0),
        'ada_w': nrm((L, 2, D, 3 * D), 0.5 * D ** -0.5),
        'ada_b': nrm((L, 2, 3 * D), 0.01),
        'norm_g': 1.0 + nrm((L, 2, D), 0.02),
        'w_in': nrm((L, D, IN_COLS), D ** -0.5),
        'pool_w': nrm((L, POOL_GROUPS, POOL_GROUP_W, POOL_GROUP_W), POOL_GROUP_W ** -0.5),
        'pool_scale': 1.0 + nrm((L, POOL_W), 0.1),
        'pool_proj': nrm((L, POOL_W, D), POOL_W ** -0.5),
        'conv_w': nrm((L, CONV_K, CONV_W), CONV_K ** -0.5),
        'conv_b': nrm((L, CONV_W), 0.01),
        'conv_ln_g': 1.0 + nrm((L, CONV_W), 0.02),
        'conv_ln_b': nrm((L, CONV_W), 0.01),
        'conv_proj': nrm((L, CONV_W, D), CONV_W ** -0.5),
        'rwkv_mu': uni((L, RWKV_COLS)),
        'rwkv_w0': w0,
        'rwkv_w2': nrm((L, W_LORA, RWKV_W), 0.5 * W_LORA ** -0.5),
        'rwkv_a0': nrm((L, RWKV_W), 0.1),
        'rwkv_a2': nrm((L, A_LORA, RWKV_W), 0.5 * A_LORA ** -0.5),
        'rwkv_g2': nrm((L, G_LORA, RWKV_W), G_LORA ** -0.5),
        'rwkv_kk_scale': 0.85 + nrm((L, RWKV_W), 0.05),
        'rwkv_ka': 1.0 + nrm((L, RWKV_W), 0.05),
        'rwkv_rk': nrm((L, RWKV_W), 0.1),
        'rwkv_lnx_g': 1.0 + nrm((L, RWKV_W), 0.02),
        'rwkv_lnx_b': nrm((L, RWKV_W), 0.01),
        'rwkv_proj': nrm((L, RWKV_W, D), RWKV_W ** -0.5),
        'vres_w_down': nrm((LV, D, V_LORA), D ** -0.5),
        'vres_mu': uni((LV, V_LORA)),
        'vres_v0': 1.0 + nrm((LV, RWKV_W), 0.1),
        'vres_v2': nrm((LV, V_LORA, RWKV_W), 0.5 * V_LORA ** -0.5),
        'w_o': nrm((L, D, D), D ** -0.5),
        'ffn_w1': nrm((N_DENSE, D, D_FF), D ** -0.5),
        'ffn_w3': nrm((N_DENSE, D, D_FF), D ** -0.5),
        'ffn_w2': nrm((N_DENSE, D_FF, D), D_FF ** -0.5),
        'moe_router': nrm((N_MOE, D, N_EXPERTS), D ** -0.5),
        'moe_w1': nrm((N_MOE, N_EXPERTS, D, D_FF_EXPERT), D ** -0.5),
        'moe_w3': nrm((N_MOE, N_EXPERTS, D, D_FF_EXPERT), D ** -0.5),
        'moe_w2': nrm((N_MOE, N_EXPERTS, D_FF_EXPERT, D), D_FF_EXPERT ** -0.5),
        'final_norm_g': 1.0 + nrm((D,), 0.02),
    }


def reference(x, c, ada_w, ada_b, norm_g, w_in, pool_w, pool_scale, pool_proj,
              conv_w, conv_b, conv_ln_g, conv_ln_b, conv_proj,
              rwkv_mu, rwkv_w0, rwkv_w2, rwkv_a0, rwkv_a2, rwkv_g2, rwkv_kk_scale, rwkv_ka,
              rwkv_rk, rwkv_lnx_g, rwkv_lnx_b, rwkv_proj,
              vres_w_down, vres_mu, vres_v0, vres_v2, w_o,
              ffn_w1, ffn_w3, ffn_w2, moe_router, moe_w1, moe_w3, moe_w2, final_norm_g):
    v_first = None
    for l in range(DEPTH):
        h, gate = _ada(x, c, ada_w[l, 0], ada_b[l, 0], norm_g[l, 0])
        if l == 0:
            w_cat = w_in[l]
        else:
            w_cat = jnp.concatenate([w_in[l], vres_w_down[l - 1]], axis=1)
        z = h @ w_cat
        y_pool = _pool_mixer(z[..., :OFF_CONV], pool_w[l], pool_scale[l], pool_proj[l])
        y_conv = _conv_module(z[..., OFF_CONV:OFF_RWKV], conv_w[l], conv_b[l],
                              conv_ln_g[l], conv_ln_b[l], conv_proj[l])
        vres = None
        if l > 0:
            zv = z[..., OFF_VRES:]
            zv = zv + (_shift(zv) - zv) * vres_mu[l - 1]
            vres = (v_first, zv, vres_v0[l - 1], vres_v2[l - 1])
        y_rwkv, v_l = _rwkv7_branch(z[..., OFF_RWKV:OFF_GATE], rwkv_mu[l], rwkv_w0[l], rwkv_w2[l],
                                    rwkv_a0[l], rwkv_a2[l], rwkv_g2[l], rwkv_kk_scale[l], rwkv_ka[l],
                                    rwkv_rk[l], rwkv_lnx_g[l], rwkv_lnx_b[l], rwkv_proj[l], vres)
        if l == 0:
            v_first = v_l
        g_p, g_c, g_r = jnp.split(jax.nn.sigmoid(z[..., OFF_GATE:OFF_VRES]), N_BRANCH, axis=-1)
        merged = g_p * y_pool + g_c * y_conv + g_r * y_rwkv
        x = x + gate * (merged @ w_o[l])
        h, gate = _ada(x, c, ada_w[l, 1], ada_b[l, 1], norm_g[l, 1])
        if l % 2 == 0:
            i = l // 2
            y = _swiglu(h, ffn_w1[i], ffn_w3[i], ffn_w2[i])
        else:
            i = l // 2
            y = _moe(h, moe_router[i], moe_w1[i], moe_w3[i], moe_w2[i])
        x = x + gate * y
    return _rms(x, final_norm_g)
```

```python
import functools

import jax
import jax.numpy as jnp
from jax import lax
from jax.experimental import pallas as pl
from jax.experimental.pallas import tpu as pltpu

F32 = jnp.float32
BF16 = jnp.bfloat16

POOL_WINDOWS = (2, 4, 8, 16)
POOL_GROUP_W = 64
POOL_W = 256
CONV_W = 256
CONV_K = 31
RWKV_HEAD = 64
RWKV_W = 512
RWKV_HEADS = 8
W_LORA = 32
A_LORA = 32
V_LORA = 16
G_LORA = 64
RWKV_COLS = 3 * RWKV_W + W_LORA + A_LORA + G_LORA
ZR_COLS = 1792
OFF_CONV = POOL_W
OFF_RWKV = OFF_CONV + 2 * CONV_W
OFF_GATE = OFF_RWKV + RWKV_COLS
N_EXPERTS = 8
RMS_EPS = 1e-6
LN_EPS = 1e-5
GN_EPS = 64e-5

LANES = 128
SUBLANES = 8
VMEM_LIMIT = 56 * 1024 * 1024
CHUNK = 64
HALO = 32


def _sigmoid(x):
    return 1.0 / (1.0 + jnp.exp(-x))


def _silu(x):
    return x * _sigmoid(x)


def _dot(a, b):
    return jnp.dot(a, b, preferred_element_type=F32)


def _ada_rms(x, mod, g):
    y = x * lax.rsqrt(jnp.mean(x * x, axis=-1, keepdims=True) + RMS_EPS) * g
    return y * (1.0 + mod[1:2, :]) + mod[0:1, :]


def _params(sem):
    return pltpu.CompilerParams(dimension_semantics=sem, vmem_limit_bytes=VMEM_LIMIT)


def _mod_kernel(c_ref, w_ref, b_ref, o_ref):
    c = c_ref[...]
    o_ref[0] = jnp.dot(_silu(c), w_ref[0], preferred_element_type=F32,
                       precision=lax.Precision.HIGHEST) + b_ref[0]


def _ada_mods(c, ada_w, ada_b):
    b_, d = c.shape
    n_sub = ada_w.shape[0] * ada_w.shape[1]
    w = ada_w.reshape(n_sub, d, 3 * d)
    bias = ada_b.reshape(n_sub, 1, 3 * d)
    c8 = jnp.zeros((SUBLANES, d), F32).at[:b_].set(c)
    tn = 1024
    out = pl.pallas_call(
        _mod_kernel,
        grid=(n_sub, 3 * d // tn),
        in_specs=[pl.BlockSpec((SUBLANES, d), lambda s, j: (0, 0)),
                  pl.BlockSpec((1, d, tn), lambda s, j: (s, 0, j)),
                  pl.BlockSpec((1, 1, tn), lambda s, j: (s, 0, j))],
        out_specs=pl.BlockSpec((1, SUBLANES, tn), lambda s, j: (s, 0, j)),
        out_shape=jax.ShapeDtypeStruct((n_sub, SUBLANES, 3 * d), F32),
        compiler_params=_params(("parallel", "parallel")),
        name="ada_mod",
    )(c8, w, bias)
    return out[:, :b_].reshape(n_sub, b_, 3, d)


def _inproj_kernel(x_ref, mod_ref, g_ref, wpc_ref, wr_ref, wg_ref, zpc_ref, zr_ref, zg_ref):
    hb = _ada_rms(x_ref[0], mod_ref[0], g_ref[...]).astype(BF16)
    zpc_ref[0] = _dot(hb, wpc_ref[...])
    zr_ref[0] = _dot(hb, wr_ref[...])
    zg_ref[0] = _sigmoid(_dot(hb, wg_ref[...])).astype(BF16)


def _in_proj(x, mod, g, wpc, wr, wg, ts=256):
    b_, s_, d = x.shape
    const = lambda b, i: (0, 0)
    tile = lambda b, i: (b, i, 0)
    return pl.pallas_call(
        _inproj_kernel,
        grid=(b_, s_ // ts),
        in_specs=[pl.BlockSpec((1, ts, d), tile),
                  pl.BlockSpec((1, 3, d), lambda b, i: (b, 0, 0)),
                  pl.BlockSpec((1, d), const),
                  pl.BlockSpec(wpc.shape, const),
                  pl.BlockSpec(wr.shape, const),
                  pl.BlockSpec(wg.shape, const)],
        out_specs=[pl.BlockSpec((1, ts, wpc.shape[1]), tile),
                   pl.BlockSpec((1, ts, wr.shape[1]), tile),
                   pl.BlockSpec((1, ts, wg.shape[1]), tile)],
        out_shape=[jax.ShapeDtypeStruct((b_, s_, wpc.shape[1]), F32),
                   jax.ShapeDtypeStruct((b_, s_, wr.shape[1]), F32),
                   jax.ShapeDtypeStruct((b_, s_, wg.shape[1]), BF16)],
        compiler_params=_params(("parallel", "parallel")),
        name="in_proj",
    )(x, mod, g, wpc, wr, wg)


def _head_sum(x):
    low = lax.broadcasted_iota(jnp.int32, (1, LANES), 1) < RWKV_HEAD
    outs = []
    for cb in range(x.shape[1] // LANES):
        xc = x[:, cb * LANES:(cb + 1) * LANES]
        s_lo = jnp.sum(jnp.where(low, xc, 0.0), axis=-1, keepdims=True)
        s_hi = jnp.sum(jnp.where(low, 0.0, xc), axis=-1, keepdims=True)
        outs.append(jnp.where(low, s_lo, s_hi))
    return jnp.concatenate(outs, axis=-1)


def _prep_kernel(has_vres, z_ref, zprev_ref, mu_ref, w0_ref, w2_ref, a0_ref, a2_ref, g2_ref,
                 ksc_ref, ka_ref, rk_ref, *rest):
    if has_vres:
        vf_ref, v0_ref, v2_ref = rest[:3]
        rest = rest[3:]
    r_ref, lw_ref, k_ref, v_ref, kk_ref, b_ref, g_ref, bonus_ref = rest

    i = pl.program_id(1)
    z = z_ref[0]
    ts = z.shape[0]
    prev = jnp.where(i > 0, zprev_ref[0, SUBLANES - 1:SUBLANES, :], 0.0)
    row = lax.broadcasted_iota(jnp.int32, (ts, 1), 0)
    zsh = jnp.where(row == 0, prev, pltpu.roll(z, shift=1, axis=0))
    z = z + (zsh - z) * mu_ref[...]

    w3 = RWKV_W
    r = z[:, 0:w3]
    k = z[:, w3:2 * w3]
    v = z[:, 2 * w3:3 * w3]
    o = 3 * w3
    wd = z[:, o:o + W_LORA]
    ad = z[:, o + W_LORA:o + W_LORA + A_LORA]
    gd = z[:, o + W_LORA + A_LORA:o + W_LORA + A_LORA + G_LORA]

    wpre = w0_ref[...] + _dot(jnp.tanh(wd).astype(BF16), w2_ref[...])
    y = -wpre
    softplus = jnp.maximum(y, 0.0) + jnp.log(1.0 + jnp.exp(-jnp.abs(y)))
    w = -softplus - 0.5
    lw_ref[0] = -jnp.exp(w)
    a = _sigmoid(a0_ref[...] + _dot(ad.astype(BF16), a2_ref[...]))
    g_ref[0] = _dot(_sigmoid(gd).astype(BF16), g2_ref[...])
    if has_vres:
        vd = z[:, RWKV_COLS:RWKV_COLS + V_LORA]
        mix = _sigmoid(v0_ref[...] + _dot(vd.astype(BF16), v2_ref[...]))
        v = v + (vf_ref[0] - v) * mix
    kk = k * ksc_ref[...]
    kk = kk / jnp.maximum(jnp.sqrt(_head_sum(kk * kk)), 1e-12)
    kh = k * (1.0 + (a - 1.0) * ka_ref[...])
    r_ref[0] = r
    k_ref[0] = kh
    v_ref[0] = v
    kk_ref[0] = kk
    b_ref[0] = kk * a
    bonus_ref[0] = _head_sum(r * kh * rk_ref[...]) * v


def _rwkv_prep(zr, mu, w0, w2, a0, a2, g2, ksc, ka, rk, vres, ts=256):
    b_, s_, zc = zr.shape
    has_vres = vres is not None
    const = lambda b, i: (0, 0)
    tile = lambda b, i: (b, i, 0)
    vec = pl.BlockSpec((1, RWKV_W), const)
    in_specs = [pl.BlockSpec((1, ts, zc), tile),
                pl.BlockSpec((1, SUBLANES, zc),
                             lambda b, i: (b, jnp.maximum(i * (ts // SUBLANES) - 1, 0), 0)),
                pl.BlockSpec((1, zc), const),
                vec, pl.BlockSpec((W_LORA, RWKV_W), const),
                vec, pl.BlockSpec((A_LORA, RWKV_W), const),
                pl.BlockSpec((G_LORA, RWKV_W), const),
                vec, vec, vec]
    args = [zr, zr, mu, w0, w2, a0, a2, g2, ksc, ka, rk]
    if has_vres:
        v_first, v0, v2 = vres
        in_specs += [pl.BlockSpec((1, ts, RWKV_W), tile), vec,
                     pl.BlockSpec((V_LORA, RWKV_W), const)]
        args += [v_first, v0, v2]
    out = jax.ShapeDtypeStruct((b_, s_, RWKV_W), F32)
    return pl.pallas_call(
        functools.partial(_prep_kernel, has_vres),
        grid=(b_, s_ // ts),
        in_specs=in_specs,
        out_specs=[pl.BlockSpec((1, ts, RWKV_W), tile)] * 8,
        out_shape=[out] * 8,
        compiler_params=_params(("parallel", "parallel")),
        name="rwkv_prep",
    )(*args)


def _split3(x):
    hi = x.astype(BF16)
    r1 = x - hi.astype(F32)
    mid = r1.astype(BF16)
    lo = (r1 - mid.astype(F32)).astype(BF16)
    return hi, mid, lo


def _wkv_kernel(r_ref, lw_ref, k_ref, v_ref, kk_ref, b_ref, o_ref, z_ref):
    @pl.when(pl.program_id(1) == 0)
    def _():
        z_ref[...] = jnp.zeros_like(z_ref)

    c_ = CHUNK
    n_ = RWKV_HEAD
    n_chunks = r_ref.shape[1] // c_
    ri = lax.broadcasted_iota(jnp.int32, (c_, c_), 0)
    ci = lax.broadcasted_iota(jnp.int32, (c_, c_), 1)
    tril_incl = (ri >= ci)
    tril_strict = (ri > ci)
    eye = (ri == ci)
    tri_b = jnp.where(tril_incl, 1.0, 0.0).astype(BF16)

    def chunk_body(ch, carry):
        rows = pl.ds(pl.multiple_of(ch * c_, c_), c_)
        lw = lw_ref[0, rows, :]
        hi, mid, lo = _split3(lw)
        cum = _dot(tri_b, hi) + _dot(tri_b, mid) + _dot(tri_b, lo)
        cum_last = cum[c_ - 1:c_, :]
        w_excl = jnp.exp(cum - lw)
        w_incl = jnp.exp(cum)
        w_inv = jnp.exp(-cum)
        w_tail = jnp.exp(cum_last - cum)
        w_chunk = jnp.exp(cum_last)
        kk = kk_ref[0, rows, :]
        bb = b_ref[0, rows, :]
        kx = k_ref[0, rows, :]
        at = (-kk * w_excl).astype(BF16)
        rt = (r_ref[0, rows, :] * w_incl).astype(BF16)
        bt = (bb * w_inv).astype(BF16)
        kt = (kx * w_inv).astype(BF16)
        bh = (bb * w_tail).astype(BF16)
        kh = (kx * w_tail).astype(BF16)
        vv = v_ref[0, rows, :].astype(BF16)
        for h in range(RWKV_HEADS):
            sl = slice(h * n_, (h + 1) * n_)
            lhs = jnp.concatenate([at[:, sl], rt[:, sl]], axis=0)
            rhs = jnp.concatenate([bt[:, sl], kt[:, sl]], axis=0)
            aa = lax.dot_general(lhs, rhs, (((1,), (1,)), ((), ())),
                                 preferred_element_type=F32)
            a_ab = jnp.where(tril_strict, aa[:c_, :c_], 0.0)
            a_ak = jnp.where(tril_strict, aa[:c_, c_:], 0.0)
            a_rb = jnp.where(tril_incl, aa[c_:, :c_], 0.0)
            a_rk = jnp.where(tril_incl, aa[c_:, c_:], 0.0)
            vh = vv[:, sl]
            av = _dot(jnp.concatenate([a_ak, a_rk], axis=0).astype(BF16), vh)
            x = jnp.concatenate([at[:, sl].astype(F32), av[:c_]], axis=1)
            p = a_ab
            for it in range(6):
                pb = p.astype(BF16)
                x = x + _dot(pb, x.astype(BF16))
                if it < 5:
                    p = _dot(pb, pb)
            xb = x.astype(BF16)
            qo = jnp.concatenate([rt[:, sl].astype(F32), av[c_:]], axis=1) + _dot(a_rb.astype(BF16), xb)
            m1 = lax.dot_general(bh[:, sl], xb, (((0,), (0,)), ((), ())),
                                 preferred_element_type=F32)
            m2 = lax.dot_general(kh[:, sl], vh, (((0,), (0,)), ((), ())),
                                 preferred_element_type=F32)
            phi_t = jnp.where(eye, w_chunk[:, sl], 0.0) + m1[:, :n_]
            psi_t = m1[:, n_:] + m2
            zb = z_ref[h].astype(BF16)
            o_ref[0, rows, sl] = _dot(qo[:, :n_].astype(BF16), zb) + qo[:, n_:]
            z_ref[h] = _dot(phi_t.astype(BF16), zb) + psi_t
        return carry

    lax.fori_loop(0, n_chunks, chunk_body, 0)


def _wkv_scan(r, lw, k, v, kk, bb, tc=512):
    b_, s_, w_ = r.shape
    tile = lambda b, i: (b, i, 0)
    spec = pl.BlockSpec((1, tc, w_), tile)
    return pl.pallas_call(
        _wkv_kernel,
        grid=(b_, s_ // tc),
        in_specs=[spec] * 6,
        out_specs=spec,
        out_shape=jax.ShapeDtypeStruct((b_, s_, w_), F32),
        scratch_shapes=[pltpu.VMEM((RWKV_HEADS, RWKV_HEAD, RWKV_HEAD), F32)],
        compiler_params=_params(("parallel", "arbitrary")),
        name="wkv_scan",
    )(r, lw, k, v, kk, bb)


def _post_kernel(x_ref, mod_ref, zpc_ref, halo_ref, zg_ref, o_ref, bonus_ref, g_ref,
                 poolw_ref, pscale_ref, pproj_ref, convw_ref, convb_ref, lng_ref, lnb_ref,
                 cproj_ref, lnxg_ref, lnxb_ref, rproj_ref, wo_ref, out_ref, ext_ref):
    i = pl.program_id(1)
    ts = x_ref.shape[1]
    halo = jnp.where(i > 0, halo_ref[0], 0.0)
    cur = zpc_ref[0]

    up = cur[:, :POOL_W]
    ext = jnp.concatenate([halo[:, :POOL_W], up], axis=0)
    e1 = ext + pltpu.roll(ext, shift=1, axis=0)
    e2 = e1 + pltpu.roll(e1, shift=2, axis=0)
    e3 = e2 + pltpu.roll(e2, shift=4, axis=0)
    e4 = e3 + pltpu.roll(e3, shift=8, axis=0)
    lane = lax.broadcasted_iota(jnp.int32, (1, POOL_W), 1)
    grp = lane // POOL_GROUP_W
    pooled = jnp.where(grp == 0, e1, jnp.where(grp == 1, e2, jnp.where(grp == 2, e3, e4)))[HALO:]
    win = jnp.where(grp == 0, 2.0, jnp.where(grp == 1, 4.0, jnp.where(grp == 2, 8.0, 16.0)))
    pos = (i * ts + 1 + lax.broadcasted_iota(jnp.int32, (ts, 1), 0)).astype(F32)
    p = pooled / jnp.minimum(pos, win) - up
    p = _dot(p.astype(BF16), poolw_ref[...]) * pscale_ref[...]
    y_pool = _dot(p.astype(BF16), pproj_ref[...])

    za = jnp.concatenate([halo[:, POOL_W:POOL_W + CONV_W], cur[:, POOL_W:POOL_W + CONV_W]], axis=0)
    zb = jnp.concatenate([halo[:, POOL_W + CONV_W:], cur[:, POOL_W + CONV_W:]], axis=0)
    ext_ref[...] = za * _sigmoid(zb)
    acc = jnp.zeros((ts, CONV_W), F32) + convb_ref[...]
    for j in range(CONV_K):
        st = HALO - (CONV_K - 1) + j
        acc = acc + ext_ref[st:st + ts, :] * convw_ref[j:j + 1, :]
    mean = jnp.mean(acc, axis=-1, keepdims=True)
    dev = acc - mean
    var = jnp.mean(dev * dev, axis=-1, keepdims=True)
    un = dev * lax.rsqrt(var + LN_EPS) * lng_ref[...] + lnb_ref[...]
    y_conv = _dot(_silu(un).astype(BF16), cproj_ref[...])

    o = o_ref[0]
    inv_n = 1.0 / RWKV_HEAD
    mu = _head_sum(o) * inv_n
    od = o - mu
    ovar = _head_sum(od * od) * inv_n
    on = od * lax.rsqrt(ovar + GN_EPS) * lnxg_ref[...] + lnxb_ref[...]
    y_rwkv = _dot(((on + bonus_ref[0]) * g_ref[0]).astype(BF16), rproj_ref[...])

    d = x_ref.shape[2]
    zg = zg_ref[0]
    merged = (zg[:, :d].astype(F32) * y_pool + zg[:, d:2 * d].astype(F32) * y_conv
              + zg[:, 2 * d:].astype(F32) * y_rwkv)
    out_ref[0] = x_ref[0] + mod_ref[0, 2:3, :] * _dot(merged.astype(BF16), wo_ref[...])


def _post(x, mod, zpc, zg, o, bonus, g, poolw, pscale, pproj, convw, convb, lng, lnb, cproj,
          lnxg, lnxb, rproj, wo, ts=256):
    b_, s_, d = x.shape
    const = lambda b, i: (0, 0)
    tile = lambda b, i: (b, i, 0)
    full = lambda a: pl.BlockSpec(a.shape, const)
    in_specs = [pl.BlockSpec((1, ts, d), tile),
                pl.BlockSpec((1, 3, d), lambda b, i: (b, 0, 0)),
                pl.BlockSpec((1, ts, zpc.shape[2]), tile),
                pl.BlockSpec((1, HALO, zpc.shape[2]),
                             lambda b, i: (b, jnp.maximum(i * (ts // HALO) - 1, 0), 0)),
                pl.BlockSpec((1, ts, zg.shape[2]), tile),
                pl.BlockSpec((1, ts, RWKV_W), tile),
                pl.BlockSpec((1, ts, RWKV_W), tile),
                pl.BlockSpec((1, ts, RWKV_W), tile)]
    weights = [poolw, pscale, pproj, convw, convb, lng, lnb, cproj, lnxg, lnxb, rproj, wo]
    in_specs += [full(a) for a in weights]
    return pl.pallas_call(
        _post_kernel,
        grid=(b_, s_ // ts),
        in_specs=in_specs,
        out_specs=pl.BlockSpec((1, ts, d), tile),
        out_shape=jax.ShapeDtypeStruct((b_, s_, d), F32),
        scratch_shapes=[pltpu.VMEM((HALO + ts, CONV_W), F32)],
        compiler_params=_params(("parallel", "parallel")),
        name="mix_merge",
    )(x, mod, zpc, zpc, zg, o, bonus, g, *weights)


def _router_kernel(x_ref, mod_ref, g_ref, wr_ref, gate_ref):
    h = _ada_rms(x_ref[0], mod_ref[0], g_ref[...])
    logits = jnp.dot(h, wr_ref[...], preferred_element_type=F32, precision=lax.Precision.HIGHEST)
    n_e = logits.shape[1]
    lane = lax.broadcasted_iota(jnp.int32, logits.shape, 1)
    m1 = jnp.max(logits, axis=-1, keepdims=True)
    i1 = jnp.min(jnp.where(logits == m1, lane, n_e), axis=-1, keepdims=True)
    sel1 = lane == i1
    rest = jnp.where(sel1, -jnp.inf, logits)
    m2 = jnp.max(rest, axis=-1, keepdims=True)
    i2 = jnp.min(jnp.where(rest == m2, lane, n_e), axis=-1, keepdims=True)
    sel2 = lane == i2
    e2 = jnp.exp(m2 - m1)
    w1 = 1.0 / (1.0 + e2)
    gate_ref[0] = jnp.where(sel1, w1, 0.0) + jnp.where(sel2, e2 * w1, 0.0)


def _router(x, mod, g, wr, ts=512):
    b_, s_, d = x.shape
    n_e = wr.shape[1]
    tile = lambda b, i: (b, i, 0)
    return pl.pallas_call(
        _router_kernel,
        grid=(b_, s_ // ts),
        in_specs=[pl.BlockSpec((1, ts, d), tile),
                  pl.BlockSpec((1, 3, d), lambda b, i: (b, 0, 0)),
                  pl.BlockSpec((1, d), lambda b, i: (0, 0)),
                  pl.BlockSpec(wr.shape, lambda b, i: (0, 0))],
        out_specs=pl.BlockSpec((1, ts, n_e), tile),
        out_shape=jax.ShapeDtypeStruct((b_, s_, n_e), F32),
        compiler_params=_params(("parallel", "parallel")),
        name="moe_router",
    )(x, mod, g, wr)


def _ffn_kernel(gated, x_ref, mod_ref, g_ref, *rest):
    if gated:
        gate_ref = rest[0]
        rest = rest[1:]
    w1_ref, w3_ref, w2_ref, out_ref, h_ref, acc_ref = rest
    e = pl.program_id(2)
    f = pl.program_id(3)

    @pl.when((e == 0) & (f == 0))
    def _():
        h_ref[...] = _ada_rms(x_ref[0], mod_ref[0], g_ref[...]).astype(BF16)
        acc_ref[...] = jnp.zeros_like(acc_ref)

    h = h_ref[...]
    act = _silu(_dot(h, w1_ref[0])) * _dot(h, w3_ref[0])
    if gated:
        gate = gate_ref[0]
        lane = lax.broadcasted_iota(jnp.int32, gate.shape, 1)
        act = act * jnp.sum(jnp.where(lane == e, gate, 0.0), axis=-1, keepdims=True)
    acc_ref[...] += _dot(act.astype(BF16), w2_ref[0])

    @pl.when((e == pl.num_programs(2) - 1) & (f == pl.num_programs(3) - 1))
    def _():
        out_ref[0] = x_ref[0] + mod_ref[0, 2:3, :] * acc_ref[...]


def _ffn(x, mod, g, w1, w3, w2, gate, tm, tf):
    b_, s_, d = x.shape
    n_e, _, ff = w1.shape
    gated = gate is not None
    tile = lambda b, i, e, f: (b, i, 0)
    in_specs = [pl.BlockSpec((1, tm, d), tile),
                pl.BlockSpec((1, 3, d), lambda b, i, e, f: (b, 0, 0)),
                pl.BlockSpec((1, d), lambda b, i, e, f: (0, 0))]
    args = [x, mod, g]
    if gated:
        in_specs.append(pl.BlockSpec((1, tm, n_e), tile))
        args.append(gate)
    in_specs += [pl.BlockSpec((1, d, tf), lambda b, i, e, f: (e, 0, f)),
                 pl.BlockSpec((1, d, tf), lambda b, i, e, f: (e, 0, f)),
                 pl.BlockSpec((1, tf, d), lambda b, i, e, f: (e, f, 0))]
    args += [w1, w3, w2]
    return pl.pallas_call(
        functools.partial(_ffn_kernel, gated),
        grid=(b_, s_ // tm, n_e, ff // tf),
        in_specs=in_specs,
        out_specs=pl.BlockSpec((1, tm, d), tile),
        out_shape=jax.ShapeDtypeStruct((b_, s_, d), F32),
        scratch_shapes=[pltpu.VMEM((tm, d), BF16), pltpu.VMEM((tm, d), F32)],
        compiler_params=_params(("parallel", "parallel", "arbitrary", "arbitrary")),
        name="moe_ffn" if gated else "ffn",
    )(*args)


def _final_kernel(x_ref, g_ref, o_ref):
    x = x_ref[0]
    o_ref[0] = x * lax.rsqrt(jnp.mean(x * x, axis=-1, keepdims=True) + RMS_EPS) * g_ref[...]


def _final_norm(x, g, ts=512):
    b_, s_, d = x.shape
    tile = lambda b, i: (b, i, 0)
    return pl.pallas_call(
        _final_kernel,
        grid=(b_, s_ // ts),
        in_specs=[pl.BlockSpec((1, ts, d), tile), pl.BlockSpec((1, d), lambda b, i: (0, 0))],
        out_specs=pl.BlockSpec((1, ts, d), tile),
        out_shape=jax.ShapeDtypeStruct((b_, s_, d), F32),
        compiler_params=_params(("parallel", "parallel")),
        name="final_norm",
    )(x, g)


def _block_diag(w):
    g_, c_, _ = w.shape
    out = jnp.zeros((g_ * c_, g_ * c_), w.dtype)
    for gi in range(g_):
        out = out.at[gi * c_:(gi + 1) * c_, gi * c_:(gi + 1) * c_].set(w[gi])
    return out


def kernel(x, c, ada_w, ada_b, norm_g, w_in, pool_w, pool_scale, pool_proj, conv_w, conv_b, conv_ln_g, conv_ln_b, conv_proj, rwkv_mu, rwkv_w0, rwkv_w2, rwkv_a0, rwkv_a2, rwkv_g2, rwkv_kk_scale, rwkv_ka, rwkv_rk, rwkv_lnx_g, rwkv_lnx_b, rwkv_proj, vres_w_down, vres_mu, vres_v0, vres_v2, w_o, ffn_w1, ffn_w3, ffn_w2, moe_router, moe_w1, moe_w3, moe_w2, final_norm_g):
    depth = w_in.shape[0]
    d = x.shape[2]
    row = lambda a: a.reshape(1, -1)
    mods = _ada_mods(c, ada_w, ada_b)
    v_first = None
    for l in range(depth):
        mod = mods[2 * l]
        zr_pad = jnp.zeros((d, ZR_COLS - RWKV_COLS - V_LORA), F32)
        mu_pad = jnp.zeros((ZR_COLS - RWKV_COLS - V_LORA,), F32)
        if l == 0:
            w_vd = jnp.zeros((d, V_LORA), F32)
            mu_vd = jnp.zeros((V_LORA,), F32)
        else:
            w_vd = vres_w_down[l - 1]
            mu_vd = vres_mu[l - 1]
        wr = jnp.concatenate([w_in[l][:, OFF_RWKV:OFF_GATE], w_vd, zr_pad], axis=1).astype(BF16)
        mu = jnp.concatenate([rwkv_mu[l], mu_vd, mu_pad]).reshape(1, ZR_COLS)
        wpc = w_in[l][:, :OFF_RWKV].astype(BF16)
        wg = w_in[l][:, OFF_GATE:OFF_GATE + 3 * d].astype(BF16)
        zpc, zr, zg = _in_proj(x, mod, row(norm_g[l, 0]), wpc, wr, wg)
        vres = None
        if l > 0:
            vres = (v_first, row(vres_v0[l - 1]), vres_v2[l - 1].astype(BF16))
        r, lw, kh, v, kk, bb, g, bonus = _rwkv_prep(
            zr, mu, row(rwkv_w0[l]), rwkv_w2[l].astype(BF16), row(rwkv_a0[l]),
            rwkv_a2[l].astype(BF16), rwkv_g2[l].astype(BF16), row(rwkv_kk_scale[l]),
            row(rwkv_ka[l]), row(rwkv_rk[l]), vres)
        if l == 0:
            v_first = v
        o = _wkv_scan(r, lw, kh, v, kk, bb)
        x = _post(x, mod, zpc, zg, o, bonus, g,
                  _block_diag(pool_w[l]).astype(BF16), row(pool_scale[l]), pool_proj[l].astype(BF16),
                  conv_w[l], row(conv_b[l]), row(conv_ln_g[l]), row(conv_ln_b[l]),
                  conv_proj[l].astype(BF16), row(rwkv_lnx_g[l]), row(rwkv_lnx_b[l]),
                  rwkv_proj[l].astype(BF16), w_o[l].astype(BF16))
        mod = mods[2 * l + 1]
        gn = row(norm_g[l, 1])
        i = l // 2
        if l % 2 == 0:
            x = _ffn(x, mod, gn, ffn_w1[i][None].astype(BF16), ffn_w3[i][None].astype(BF16),
                     ffn_w2[i][None].astype(BF16), None, tm=512, tf=1408)
        else:
            gate = _router(x, mod, gn, moe_router[i])
            x = _ffn(x, mod, gn, moe_w1[i].astype(BF16), moe_w3[i].astype(BF16),
                     moe_w2[i].astype(BF16), gate, tm=512, tf=896)
    return _final_norm(x, row(final_norm_g))
```

```python
import functools

import jax
import jax.numpy as jnp
from jax import lax
from jax.experimental import pallas as pl
from jax.experimental.pallas import tpu as pltpu

F32 = jnp.float32
BF16 = jnp.bfloat16

POOL_WINDOWS = (2, 4, 8, 16)
POOL_GROUP_W = 64
POOL_W = 256
CONV_W = 256
CONV_K = 31
RWKV_HEAD = 64
RWKV_W = 512
RWKV_HEADS = 8
W_LORA = 32
A_LORA = 32
V_LORA = 16
G_LORA = 64
RWKV_COLS = 3 * RWKV_W + W_LORA + A_LORA + G_LORA
ZR_COLS = 1792
OFF_CONV = POOL_W
OFF_RWKV = OFF_CONV + 2 * CONV_W
OFF_GATE = OFF_RWKV + RWKV_COLS
N_EXPERTS = 8
RMS_EPS = 1e-6
LN_EPS = 1e-5
GN_EPS = 64e-5

LANES = 128
SUBLANES = 8
VMEM_LIMIT = 56 * 1024 * 1024
CHUNK = 64
HALO = 32


def _sigmoid(x):
    return 1.0 / (1.0 + jnp.exp(-x))


def _silu(x):
    return x * _sigmoid(x)


def _dot(a, b):
    return jnp.dot(a, b, preferred_element_type=F32)


def _ada_rms(x, mod, g):
    y = x * lax.rsqrt(jnp.mean(x * x, axis=-1, keepdims=True) + RMS_EPS) * g
    return y * (1.0 + mod[1:2, :]) + mod[0:1, :]


def _params(sem):
    return pltpu.CompilerParams(dimension_semantics=sem, vmem_limit_bytes=VMEM_LIMIT)


def _mod_kernel(c_ref, w_ref, b_ref, o_ref):
    c = c_ref[...]
    o_ref[0] = jnp.dot(_silu(c), w_ref[0], preferred_element_type=F32,
                       precision=lax.Precision.HIGHEST) + b_ref[0]


def _ada_mods(c, ada_w, ada_b):
    b_, d = c.shape
    n_sub = ada_w.shape[0] * ada_w.shape[1]
    w = ada_w.reshape(n_sub, d, 3 * d)
    bias = ada_b.reshape(n_sub, 1, 3 * d)
    c8 = jnp.zeros((SUBLANES, d), F32).at[:b_].set(c)
    tn = 1024
    out = pl.pallas_call(
        _mod_kernel,
        grid=(n_sub, 3 * d // tn),
        in_specs=[pl.BlockSpec((SUBLANES, d), lambda s, j: (0, 0)),
                  pl.BlockSpec((1, d, tn), lambda s, j: (s, 0, j)),
                  pl.BlockSpec((1, 1, tn), lambda s, j: (s, 0, j))],
        out_specs=pl.BlockSpec((1, SUBLANES, tn), lambda s, j: (s, 0, j)),
        out_shape=jax.ShapeDtypeStruct((n_sub, SUBLANES, 3 * d), F32),
        compiler_params=_params(("parallel", "parallel")),
        name="ada_mod",
    )(c8, w, bias)
    return out[:, :b_].reshape(n_sub, b_, 3, d)


def _inproj_kernel(x_ref, mod_ref, g_ref, wpc_ref, wr_ref, wg_ref, zpc_ref, zr_ref, zg_ref):
    hb = _ada_rms(x_ref[0], mod_ref[0], g_ref[...]).astype(BF16)
    zpc_ref[0] = _dot(hb, wpc_ref[...])
    zr_ref[0] = _dot(hb, wr_ref[...])
    zg_ref[0] = _sigmoid(_dot(hb, wg_ref[...])).astype(BF16)


def _in_proj(x, mod, g, wpc, wr, wg, ts=256):
    b_, s_, d = x.shape
    const = lambda b, i: (0, 0)
    tile = lambda b, i: (b, i, 0)
    return pl.pallas_call(
        _inproj_kernel,
        grid=(b_, s_ // ts),
        in_specs=[pl.BlockSpec((1, ts, d), tile),
                  pl.BlockSpec((1, 3, d), lambda b, i: (b, 0, 0)),
                  pl.BlockSpec((1, d), const),
                  pl.BlockSpec(wpc.shape, const),
                  pl.BlockSpec(wr.shape, const),
                  pl.BlockSpec(wg.shape, const)],
        out_specs=[pl.BlockSpec((1, ts, wpc.shape[1]), tile),
                   pl.BlockSpec((1, ts, wr.shape[1]), tile),
                   pl.BlockSpec((1, ts, wg.shape[1]), tile)],
        out_shape=[jax.ShapeDtypeStruct((b_, s_, wpc.shape[1]), F32),
                   jax.ShapeDtypeStruct((b_, s_, wr.shape[1]), F32),
                   jax.ShapeDtypeStruct((b_, s_, wg.shape[1]), BF16)],
        compiler_params=_params(("parallel", "parallel")),
        name="in_proj",
    )(x, mod, g, wpc, wr, wg)


def _head_sum(x):
    low = lax.broadcasted_iota(jnp.int32, (1, LANES), 1) < RWKV_HEAD
    outs = []
    for cb in range(x.shape[1] // LANES):
        xc = x[:, cb * LANES:(cb + 1) * LANES]
        s_lo = jnp.sum(jnp.where(low, xc, 0.0), axis=-1, keepdims=True)
        s_hi = jnp.sum(jnp.where(low, 0.0, xc), axis=-1, keepdims=True)
        outs.append(jnp.where(low, s_lo, s_hi))
    return jnp.concatenate(outs, axis=-1)


def _prep_kernel(has_vres, z_ref, zprev_ref, mu_ref, w0_ref, w2_ref, a0_ref, a2_ref, g2_ref,
                 ksc_ref, ka_ref, rk_ref, *rest):
    if has_vres:
        vf_ref, v0_ref, v2_ref = rest[:3]
        rest = rest[3:]
    r_ref, lw_ref, k_ref, v_ref, kk_ref, b_ref, g_ref, bonus_ref = rest

    i = pl.program_id(1)
    z = z_ref[0]
    ts = z.shape[0]
    prev = jnp.where(i > 0, zprev_ref[0, SUBLANES - 1:SUBLANES, :], 0.0)
    row = lax.broadcasted_iota(jnp.int32, (ts, 1), 0)
    zsh = jnp.where(row == 0, prev, pltpu.roll(z, shift=1, axis=0))
    z = z + (zsh - z) * mu_ref[...]

    w3 = RWKV_W
    r = z[:, 0:w3]
    k = z[:, w3:2 * w3]
    v = z[:, 2 * w3:3 * w3]
    o = 3 * w3
    wd = z[:, o:o + W_LORA]
    ad = z[:, o + W_LORA:o + W_LORA + A_LORA]
    gd = z[:, o + W_LORA + A_LORA:o + W_LORA + A_LORA + G_LORA]

    wpre = w0_ref[...] + _dot(jnp.tanh(wd).astype(BF16), w2_ref[...])
    y = -wpre
    softplus = jnp.maximum(y, 0.0) + jnp.log(1.0 + jnp.exp(-jnp.abs(y)))
    w = -softplus - 0.5
    lw_ref[0] = -jnp.exp(w)
    a = _sigmoid(a0_ref[...] + _dot(ad.astype(BF16), a2_ref[...]))
    g_ref[0] = _dot(_sigmoid(gd).astype(BF16), g2_ref[...])
    if has_vres:
        vd = z[:, RWKV_COLS:RWKV_COLS + V_LORA]
        mix = _sigmoid(v0_ref[...] + _dot(vd.astype(BF16), v2_ref[...]))
        v = v + (vf_ref[0] - v) * mix
    kk = k * ksc_ref[...]
    kk = kk / jnp.maximum(jnp.sqrt(_head_sum(kk * kk)), 1e-12)
    kh = k * (1.0 + (a - 1.0) * ka_ref[...])
    r_ref[0] = r
    k_ref[0] = kh
    v_ref[0] = v
    kk_ref[0] = kk
    b_ref[0] = kk * a
    bonus_ref[0] = _head_sum(r * kh * rk_ref[...]) * v


def _rwkv_prep(zr, mu, w0, w2, a0, a2, g2, ksc, ka, rk, vres, ts=256):
    b_, s_, zc = zr.shape
    has_vres = vres is not None
    const = lambda b, i: (0, 0)
    tile = lambda b, i: (b, i, 0)
    vec = pl.BlockSpec((1, RWKV_W), const)
    in_specs = [pl.BlockSpec((1, ts, zc), tile),
                pl.BlockSpec((1, SUBLANES, zc),
                             lambda b, i: (b, jnp.maximum(i * (ts // SUBLANES) - 1, 0), 0)),
                pl.BlockSpec((1, zc), const),
                vec, pl.BlockSpec((W_LORA, RWKV_W), const),
                vec, pl.BlockSpec((A_LORA, RWKV_W), const),
                pl.BlockSpec((G_LORA, RWKV_W), const),
                vec, vec, vec]
    args = [zr, zr, mu, w0, w2, a0, a2, g2, ksc, ka, rk]
    if has_vres:
        v_first, v0, v2 = vres
        in_specs += [pl.BlockSpec((1, ts, RWKV_W), tile), vec,
                     pl.BlockSpec((V_LORA, RWKV_W), const)]
        args += [v_first, v0, v2]
    out = jax.ShapeDtypeStruct((b_, s_, RWKV_W), F32)
    return pl.pallas_call(
        functools.partial(_prep_kernel, has_vres),
        grid=(b_, s_ // ts),
        in_specs=in_specs,
        out_specs=[pl.BlockSpec((1, ts, RWKV_W), tile)] * 8,
        out_shape=[out] * 8,
        compiler_params=_params(("parallel", "parallel")),
        name="rwkv_prep",
    )(*args)


def _split3(x):
    hi = x.astype(BF16)
    r1 = x - hi.astype(F32)
    mid = r1.astype(BF16)
    lo = (r1 - mid.astype(F32)).astype(BF16)
    return hi, mid, lo


def _dot_nt(a, b):
    return lax.dot_general(a, b, (((1,), (1,)), ((), ())), preferred_element_type=F32)


def _dot_tn(a, b):
    return lax.dot_general(a, b, (((0,), (0,)), ((), ())), preferred_element_type=F32)


def _wkv_kernel(r_ref, lw_ref, k_ref, v_ref, kk_ref, b_ref, o_ref, z_ref):
    @pl.when(pl.program_id(1) == 0)
    def _():
        z_ref[...] = jnp.zeros_like(z_ref)

    c_ = CHUNK
    n_ = RWKV_HEAD
    n_pairs = RWKV_HEADS // 2
    n_chunks = r_ref.shape[1] // c_
    ri = lax.broadcasted_iota(jnp.int32, (c_, c_), 0)
    ci = lax.broadcasted_iota(jnp.int32, (c_, c_), 1)
    tril_incl = (ri >= ci)
    tril_strict = (ri > ci)
    tri_b = jnp.where(tril_incl, 1.0, 0.0).astype(BF16)
    rl = lax.broadcasted_iota(jnp.int32, (LANES, LANES), 0)
    cl = lax.broadcasted_iota(jnp.int32, (LANES, LANES), 1)
    eye_pair = (rl == cl)
    same_head = ((rl < n_) == (cl < n_))
    head0 = lax.broadcasted_iota(jnp.int32, (1, LANES), 1) < n_
    head0_x2 = jnp.concatenate([head0, head0], axis=1)

    def chunk_body(ch, carry):
        rows = pl.ds(pl.multiple_of(ch * c_, c_), c_)
        lw = lw_ref[0, rows, :]
        hi, mid, lo = _split3(lw)
        cum = _dot(tri_b, hi) + _dot(tri_b, mid) + _dot(tri_b, lo)
        cum_last = cum[c_ - 1:c_, :]
        w_inv = jnp.exp(-cum)
        w_tail = jnp.exp(cum_last - cum)
        w_chunk = jnp.exp(cum_last)
        kk = kk_ref[0, rows, :]
        bb = b_ref[0, rows, :]
        kx = k_ref[0, rows, :]
        at32 = -kk * jnp.exp(cum - lw)
        rt32 = r_ref[0, rows, :] * jnp.exp(cum)
        at = at32.astype(BF16)
        rt = rt32.astype(BF16)
        bt = (bb * w_inv).astype(BF16)
        kt = (kx * w_inv).astype(BF16)
        bh = (bb * w_tail).astype(BF16)
        kh = (kx * w_tail).astype(BF16)
        vv = v_ref[0, rows, :].astype(BF16)

        chains = [(p, h) for p in range(n_pairs) for h in range(2)]
        pl_ = lambda a, p: a[:, p * LANES:(p + 1) * LANES]
        aa = []
        for p, h in chains:
            lhs = jnp.concatenate([pl_(at, p), pl_(rt, p)], axis=0)
            rhs = jnp.concatenate([pl_(bt, p), pl_(kt, p)], axis=0)
            keep = head0 if h == 0 else jnp.logical_not(head0)
            aa.append(_dot_nt(jnp.where(keep, lhs, jnp.zeros_like(lhs)), rhs))
        a_ab = [jnp.where(tril_strict, a[:c_, :c_], 0.0).astype(BF16) for a in aa]
        a_rb = [jnp.where(tril_incl, a[c_:, :c_], 0.0).astype(BF16) for a in aa]
        a_k = [jnp.concatenate([jnp.where(tril_strict, a[:c_, c_:], 0.0),
                                jnp.where(tril_incl, a[c_:, c_:], 0.0)], axis=0).astype(BF16)
               for a in aa]
        av = [_dot(a_k[i], pl_(vv, p)) for i, (p, h) in enumerate(chains)]
        xs = [jnp.concatenate([pl_(at32, p), av[i][:c_]], axis=1)
              for i, (p, h) in enumerate(chains)]
        ps = a_ab
        for it in range(6):
            xs = [x + _dot(pw, x.astype(BF16)) for x, pw in zip(xs, ps)]
            if it < 5:
                ps = [_dot(pw, pw).astype(BF16) for pw in ps]
        qos = [jnp.concatenate([pl_(rt32, p), av[i][c_:]], axis=1)
               + _dot(a_rb[i], xs[i].astype(BF16)) for i, (p, h) in enumerate(chains)]
        for p in range(n_pairs):
            xp = jnp.where(head0_x2, xs[2 * p], xs[2 * p + 1]).astype(BF16)
            qo = jnp.where(head0_x2, qos[2 * p], qos[2 * p + 1])
            m1 = _dot_tn(pl_(bh, p), xp)
            m2 = _dot_tn(pl_(kh, p), pl_(vv, p))
            phi_t = jnp.where(eye_pair, pl_(w_chunk, p), 0.0) + jnp.where(same_head, m1[:, :LANES], 0.0)
            psi_t = jnp.where(same_head, m1[:, LANES:] + m2, 0.0)
            zb = z_ref[p].astype(BF16)
            o_ref[0, rows, p * LANES:(p + 1) * LANES] = _dot(qo[:, :LANES].astype(BF16), zb) + qo[:, LANES:]
            z_ref[p] = _dot(phi_t.astype(BF16), zb) + psi_t
        return carry

    lax.fori_loop(0, n_chunks, chunk_body, 0)


def _wkv_scan(r, lw, k, v, kk, bb, tc=512):
    b_, s_, w_ = r.shape
    tile = lambda b, i: (b, i, 0)
    spec = pl.BlockSpec((1, tc, w_), tile)
    return pl.pallas_call(
        _wkv_kernel,
        grid=(b_, s_ // tc),
        in_specs=[spec] * 6,
        out_specs=spec,
        out_shape=jax.ShapeDtypeStruct((b_, s_, w_), F32),
        scratch_shapes=[pltpu.VMEM((RWKV_HEADS // 2, LANES, LANES), F32)],
        compiler_params=_params(("parallel", "arbitrary")),
        name="wkv_scan",
    )(r, lw, k, v, kk, bb)


def _post_kernel(x_ref, mod_ref, zpc_ref, halo_ref, zg_ref, o_ref, bonus_ref, g_ref,
                 poolw_ref, pscale_ref, pproj_ref, convw_ref, convb_ref, lng_ref, lnb_ref,
                 cproj_ref, lnxg_ref, lnxb_ref, rproj_ref, wo_ref, out_ref, ext_ref):
    i = pl.program_id(1)
    ts = x_ref.shape[1]
    halo = jnp.where(i > 0, halo_ref[0], 0.0)
    cur = zpc_ref[0]

    up = cur[:, :POOL_W]
    ext = jnp.concatenate([halo[:, :POOL_W], up], axis=0)
    e1 = ext + pltpu.roll(ext, shift=1, axis=0)
    e2 = e1 + pltpu.roll(e1, shift=2, axis=0)
    e3 = e2 + pltpu.roll(e2, shift=4, axis=0)
    e4 = e3 + pltpu.roll(e3, shift=8, axis=0)
    lane = lax.broadcasted_iota(jnp.int32, (1, POOL_W), 1)
    grp = lane // POOL_GROUP_W
    pooled = jnp.where(grp == 0, e1, jnp.where(grp == 1, e2, jnp.where(grp == 2, e3, e4)))[HALO:]
    win = jnp.where(grp == 0, 2.0, jnp.where(grp == 1, 4.0, jnp.where(grp == 2, 8.0, 16.0)))
    pos = (i * ts + 1 + lax.broadcasted_iota(jnp.int32, (ts, 1), 0)).astype(F32)
    p = pooled / jnp.minimum(pos, win) - up
    p = _dot(p.astype(BF16), poolw_ref[...]) * pscale_ref[...]
    y_pool = _dot(p.astype(BF16), pproj_ref[...])

    za = jnp.concatenate([halo[:, POOL_W:POOL_W + CONV_W], cur[:, POOL_W:POOL_W + CONV_W]], axis=0)
    zb = jnp.concatenate([halo[:, POOL_W + CONV_W:], cur[:, POOL_W + CONV_W:]], axis=0)
    ext_ref[...] = za * _sigmoid(zb)
    acc = jnp.zeros((ts, CONV_W), F32) + convb_ref[...]
    for j in range(CONV_K):
        st = HALO - (CONV_K - 1) + j
        acc = acc + ext_ref[st:st + ts, :] * convw_ref[j:j + 1, :]
    mean = jnp.mean(acc, axis=-1, keepdims=True)
    dev = acc - mean
    var = jnp.mean(dev * dev, axis=-1, keepdims=True)
    un = dev * lax.rsqrt(var + LN_EPS) * lng_ref[...] + lnb_ref[...]
    y_conv = _dot(_silu(un).astype(BF16), cproj_ref[...])

    o = o_ref[0]
    inv_n = 1.0 / RWKV_HEAD
    mu = _head_sum(o) * inv_n
    od = o - mu
    ovar = _head_sum(od * od) * inv_n
    on = od * lax.rsqrt(ovar + GN_EPS) * lnxg_ref[...] + lnxb_ref[...]
    y_rwkv = _dot(((on + bonus_ref[0]) * g_ref[0]).astype(BF16), rproj_ref[...])

    d = x_ref.shape[2]
    zg = zg_ref[0]
    merged = (zg[:, :d].astype(F32) * y_pool + zg[:, d:2 * d].astype(F32) * y_conv
              + zg[:, 2 * d:].astype(F32) * y_rwkv)
    out_ref[0] = x_ref[0] + mod_ref[0, 2:3, :] * _dot(merged.astype(BF16), wo_ref[...])


def _post(x, mod, zpc, zg, o, bonus, g, poolw, pscale, pproj, convw, convb, lng, lnb, cproj,
          lnxg, lnxb, rproj, wo, ts=256):
    b_, s_, d = x.shape
    const = lambda b, i: (0, 0)
    tile = lambda b, i: (b, i, 0)
    full = lambda a: pl.BlockSpec(a.shape, const)
    in_specs = [pl.BlockSpec((1, ts, d), tile),
                pl.BlockSpec((1, 3, d), lambda b, i: (b, 0, 0)),
                pl.BlockSpec((1, ts, zpc.shape[2]), tile),
                pl.BlockSpec((1, HALO, zpc.shape[2]),
                             lambda b, i: (b, jnp.maximum(i * (ts // HALO) - 1, 0), 0)),
                pl.BlockSpec((1, ts, zg.shape[2]), tile),
                pl.BlockSpec((1, ts, RWKV_W), tile),
                pl.BlockSpec((1, ts, RWKV_W), tile),
                pl.BlockSpec((1, ts, RWKV_W), tile)]
    weights = [poolw, pscale, pproj, convw, convb, lng, lnb, cproj, lnxg, lnxb, rproj, wo]
    in_specs += [full(a) for a in weights]
    return pl.pallas_call(
        _post_kernel,
        grid=(b_, s_ // ts),
        in_specs=in_specs,
        out_specs=pl.BlockSpec((1, ts, d), tile),
        out_shape=jax.ShapeDtypeStruct((b_, s_, d), F32),
        scratch_shapes=[pltpu.VMEM((HALO + ts, CONV_W), F32)],
        compiler_params=_params(("parallel", "parallel")),
        name="mix_merge",
    )(x, mod, zpc, zpc, zg, o, bonus, g, *weights)


def _router_kernel(x_ref, mod_ref, g_ref, wr_ref, gate_ref, h_ref):
    h = _ada_rms(x_ref[0], mod_ref[0], g_ref[...])
    h_ref[0] = h.astype(BF16)
    logits = jnp.dot(h, wr_ref[...], preferred_element_type=F32, precision=lax.Precision.HIGHEST)
    n_e = logits.shape[1]
    lane = lax.broadcasted_iota(jnp.int32, logits.shape, 1)
    m1 = jnp.max(logits, axis=-1, keepdims=True)
    i1 = jnp.min(jnp.where(logits == m1, lane, n_e), axis=-1, keepdims=True)
    sel1 = lane == i1
    rest = jnp.where(sel1, -jnp.inf, logits)
    m2 = jnp.max(rest, axis=-1, keepdims=True)
    i2 = jnp.min(jnp.where(rest == m2, lane, n_e), axis=-1, keepdims=True)
    sel2 = lane == i2
    e2 = jnp.exp(m2 - m1)
    w1 = 1.0 / (1.0 + e2)
    gate_ref[0] = jnp.where(sel1, w1, 0.0) + jnp.where(sel2, e2 * w1, 0.0)


def _router(x, mod, g, wr, ts=512):
    b_, s_, d = x.shape
    n_e = wr.shape[1]
    tile = lambda b, i: (b, i, 0)
    return pl.pallas_call(
        _router_kernel,
        grid=(b_, s_ // ts),
        in_specs=[pl.BlockSpec((1, ts, d), tile),
                  pl.BlockSpec((1, 3, d), lambda b, i: (b, 0, 0)),
                  pl.BlockSpec((1, d), lambda b, i: (0, 0)),
                  pl.BlockSpec(wr.shape, lambda b, i: (0, 0))],
        out_specs=[pl.BlockSpec((1, ts, n_e), tile), pl.BlockSpec((1, ts, d), tile)],
        out_shape=[jax.ShapeDtypeStruct((b_, s_, n_e), F32), jax.ShapeDtypeStruct((b_, s_, d), BF16)],
        compiler_params=_params(("parallel", "parallel")),
        name="moe_router",
    )(x, mod, g, wr)


PIECE = 128


def _moe_kernel(cnt_ref, h_ref, rank_em_ref, rank_tm_ref, gate_ref, w1_ref, w3_ref, w2_ref,
                y_ref, xs_ref, acc_ref):
    b = pl.program_id(0)
    e = pl.program_id(1)
    f = pl.program_id(2)
    n_e = pl.num_programs(1)
    n_pieces = (cnt_ref[b * n_e + e] + (PIECE - 1)) // PIECE
    tb = h_ref.shape[0]
    piece_rows = lambda j: pl.ds(pl.multiple_of(j * PIECE, PIECE), PIECE)

    @pl.when((e == 0) & (f == 0))
    def _():
        y_ref[...] = jnp.zeros_like(y_ref)

    @pl.when(f == 0)
    def _():
        rank_row = rank_em_ref[0, 0]
        slot = lax.broadcasted_iota(jnp.int32, (PIECE, 1), 0)

        def gather(j, carry):
            onehot = jnp.where(rank_row == slot + j * PIECE, 1.0, 0.0).astype(BF16)
            xs_ref[piece_rows(j), :] = _dot(onehot, h_ref[...]).astype(BF16)
            return carry
        lax.fori_loop(0, n_pieces, gather, 0)

    def expert(j, carry):
        rows = piece_rows(j)
        xj = xs_ref[rows, :]
        act = _silu(_dot(xj, w1_ref[0])) * _dot(xj, w3_ref[0])
        part = _dot(act.astype(BF16), w2_ref[0])

        @pl.when(f == 0)
        def _():
            acc_ref[rows, :] = part

        @pl.when(f > 0)
        def _():
            acc_ref[rows, :] += part
        return carry
    lax.fori_loop(0, n_pieces, expert, 0)

    @pl.when(f == pl.num_programs(2) - 1)
    def _():
        lane_e = lax.broadcasted_iota(jnp.int32, (1, n_e), 1) == e
        rank_col = jnp.sum(jnp.where(lane_e, rank_tm_ref[...], 0), axis=-1, keepdims=True)
        gate_col = jnp.sum(jnp.where(lane_e, gate_ref[...], 0.0), axis=-1, keepdims=True)
        slot = lax.broadcasted_iota(jnp.int32, (1, PIECE), 1)

        def scatter(j, carry):
            onehot_t = jnp.where(rank_col == slot + j * PIECE, 1.0, 0.0).astype(BF16)
            y_ref[...] += gate_col * _dot(onehot_t, acc_ref[piece_rows(j), :].astype(BF16))
            return carry
        lax.fori_loop(0, n_pieces, scatter, 0)


def _moe(h, gate, w1, w3, w2, tb=1024, tf=1792):
    t_, d = h.shape
    n_e, _, ff = w1.shape
    nb = t_ // tb
    sel = (gate != 0.0).reshape(nb, tb, n_e)
    rank_tm = jnp.where(sel, jnp.cumsum(sel.astype(jnp.int32), axis=1) - 1, -1)
    cnt = jnp.sum(sel.astype(jnp.int32), axis=1).reshape(nb * n_e)
    rank_em = jnp.transpose(rank_tm, (0, 2, 1)).reshape(nb, n_e, 1, tb)
    grid_spec = pltpu.PrefetchScalarGridSpec(
        num_scalar_prefetch=1,
        grid=(nb, n_e, ff // tf),
        in_specs=[pl.BlockSpec((tb, d), lambda b, e, f, c: (b, 0)),
                  pl.BlockSpec((1, 1, 1, tb), lambda b, e, f, c: (b, e, 0, 0)),
                  pl.BlockSpec((tb, n_e), lambda b, e, f, c: (b, 0)),
                  pl.BlockSpec((tb, n_e), lambda b, e, f, c: (b, 0)),
                  pl.BlockSpec((1, d, tf), lambda b, e, f, c: (e, 0, f)),
                  pl.BlockSpec((1, d, tf), lambda b, e, f, c: (e, 0, f)),
                  pl.BlockSpec((1, tf, d), lambda b, e, f, c: (e, f, 0))],
        out_specs=pl.BlockSpec((tb, d), lambda b, e, f, c: (b, 0)),
        scratch_shapes=[pltpu.VMEM((tb, d), BF16), pltpu.VMEM((tb, d), F32)])
    return pl.pallas_call(
        _moe_kernel,
        grid_spec=grid_spec,
        out_shape=jax.ShapeDtypeStruct((t_, d), F32),
        compiler_params=_params(("parallel", "arbitrary", "arbitrary")),
        name="moe_sparse",
    )(cnt, h, rank_em, rank_tm.reshape(t_, n_e), gate, w1, w3, w2)


def _ffn_kernel(x_ref, mod_ref, g_ref, w1_ref, w3_ref, w2_ref, out_ref, h_ref, acc_ref):
    f = pl.program_id(2)

    @pl.when(f == 0)
    def _():
        h_ref[...] = _ada_rms(x_ref[0], mod_ref[0], g_ref[...]).astype(BF16)
        acc_ref[...] = jnp.zeros_like(acc_ref)

    h = h_ref[...]
    act = _silu(_dot(h, w1_ref[...])) * _dot(h, w3_ref[...])
    acc_ref[...] += _dot(act.astype(BF16), w2_ref[...])

    @pl.when(f == pl.num_programs(2) - 1)
    def _():
        out_ref[0] = x_ref[0] + mod_ref[0, 2:3, :] * acc_ref[...]


def _ffn(x, mod, g, w1, w3, w2, tm=512, tf=1408):
    b_, s_, d = x.shape
    ff = w1.shape[1]
    tile = lambda b, i, f: (b, i, 0)
    return pl.pallas_call(
        _ffn_kernel,
        grid=(b_, s_ // tm, ff // tf),
        in_specs=[pl.BlockSpec((1, tm, d), tile),
                  pl.BlockSpec((1, 3, d), lambda b, i, f: (b, 0, 0)),
                  pl.BlockSpec((1, d), lambda b, i, f: (0, 0)),
                  pl.BlockSpec((d, tf), lambda b, i, f: (0, f)),
                  pl.BlockSpec((d, tf), lambda b, i, f: (0, f)),
                  pl.BlockSpec((tf, d), lambda b, i, f: (f, 0))],
        out_specs=pl.BlockSpec((1, tm, d), tile),
        out_shape=jax.ShapeDtypeStruct((b_, s_, d), F32),
        scratch_shapes=[pltpu.VMEM((tm, d), BF16), pltpu.VMEM((tm, d), F32)],
        compiler_params=_params(("parallel", "parallel", "arbitrary")),
        name="ffn",
    )(x, mod, g, w1, w3, w2)


def _finish_kernel(has_y, has_norm, x_ref, *rest):
    x = x_ref[0]
    if has_y:
        y_ref, mod_ref = rest[:2]
        rest = rest[2:]
        x = x + mod_ref[0, 2:3, :] * y_ref[0]
    if has_norm:
        g_ref = rest[0]
        rest = rest[1:]
        x = x * lax.rsqrt(jnp.mean(x * x, axis=-1, keepdims=True) + RMS_EPS) * g_ref[...]
    rest[0][0] = x


def _finish(x, y=None, mod=None, g=None, ts=512):
    b_, s_, d = x.shape
    tile = lambda b, i: (b, i, 0)
    in_specs = [pl.BlockSpec((1, ts, d), tile)]
    args = [x]
    if y is not None:
        in_specs += [pl.BlockSpec((1, ts, d), tile), pl.BlockSpec((1, 3, d), lambda b, i: (b, 0, 0))]
        args += [y, mod]
    if g is not None:
        in_specs.append(pl.BlockSpec((1, d), lambda b, i: (0, 0)))
        args.append(g)
    return pl.pallas_call(
        functools.partial(_finish_kernel, y is not None, g is not None),
        grid=(b_, s_ // ts),
        in_specs=in_specs,
        out_specs=pl.BlockSpec((1, ts, d), tile),
        out_shape=jax.ShapeDtypeStruct((b_, s_, d), F32),
        compiler_params=_params(("parallel", "parallel")),
        name="finish",
    )(*args)


def _block_diag(w):
    g_, c_, _ = w.shape
    out = jnp.zeros((g_ * c_, g_ * c_), w.dtype)
    for gi in range(g_):
        out = out.at[gi * c_:(gi + 1) * c_, gi * c_:(gi + 1) * c_].set(w[gi])
    return out


def kernel(x, c, ada_w, ada_b, norm_g, w_in, pool_w, pool_scale, pool_proj, conv_w, conv_b, conv_ln_g, conv_ln_b, conv_proj, rwkv_mu, rwkv_w0, rwkv_w2, rwkv_a0, rwkv_a2, rwkv_g2, rwkv_kk_scale, rwkv_ka, rwkv_rk, rwkv_lnx_g, rwkv_lnx_b, rwkv_proj, vres_w_down, vres_mu, vres_v0, vres_v2, w_o, ffn_w1, ffn_w3, ffn_w2, moe_router, moe_w1, moe_w3, moe_w2, final_norm_g):
    depth = w_in.shape[0]
    d = x.shape[2]
    row = lambda a: a.reshape(1, -1)
    mods = _ada_mods(c, ada_w, ada_b)
    v_first = None
    for l in range(depth):
        mod = mods[2 * l]
        zr_pad = jnp.zeros((d, ZR_COLS - RWKV_COLS - V_LORA), F32)
        mu_pad = jnp.zeros((ZR_COLS - RWKV_COLS - V_LORA,), F32)
        if l == 0:
            w_vd = jnp.zeros((d, V_LORA), F32)
            mu_vd = jnp.zeros((V_LORA,), F32)
        else:
            w_vd = vres_w_down[l - 1]
            mu_vd = vres_mu[l - 1]
        wr = jnp.concatenate([w_in[l][:, OFF_RWKV:OFF_GATE], w_vd, zr_pad], axis=1).astype(BF16)
        mu = jnp.concatenate([rwkv_mu[l], mu_vd, mu_pad]).reshape(1, ZR_COLS)
        wpc = w_in[l][:, :OFF_RWKV].astype(BF16)
        wg = w_in[l][:, OFF_GATE:OFF_GATE + 3 * d].astype(BF16)
        zpc, zr, zg = _in_proj(x, mod, row(norm_g[l, 0]), wpc, wr, wg)
        vres = None
        if l > 0:
            vres = (v_first, row(vres_v0[l - 1]), vres_v2[l - 1].astype(BF16))
        r, lw, kh, v, kk, bb, g, bonus = _rwkv_prep(
            zr, mu, row(rwkv_w0[l]), rwkv_w2[l].astype(BF16), row(rwkv_a0[l]),
            rwkv_a2[l].astype(BF16), rwkv_g2[l].astype(BF16), row(rwkv_kk_scale[l]),
            row(rwkv_ka[l]), row(rwkv_rk[l]), vres)
        if l == 0:
            v_first = v
        o = _wkv_scan(r, lw, kh, v, kk, bb)
        x = _post(x, mod, zpc, zg, o, bonus, g,
                  _block_diag(pool_w[l]).astype(BF16), row(pool_scale[l]), pool_proj[l].astype(BF16),
                  conv_w[l], row(conv_b[l]), row(conv_ln_g[l]), row(conv_ln_b[l]),
                  conv_proj[l].astype(BF16), row(rwkv_lnx_g[l]), row(rwkv_lnx_b[l]),
                  rwkv_proj[l].astype(BF16), w_o[l].astype(BF16))
        mod = mods[2 * l + 1]
        gn = row(norm_g[l, 1])
        i = l // 2
        last = l == depth - 1
        if l % 2 == 0:
            x = _ffn(x, mod, gn, ffn_w1[i].astype(BF16), ffn_w3[i].astype(BF16), ffn_w2[i].astype(BF16))
            if last:
                x = _finish(x, g=row(final_norm_g))
        else:
            gate, h = _router(x, mod, gn, moe_router[i])
            n_e = gate.shape[2]
            y = _moe(h.reshape(-1, d), gate.reshape(-1, n_e), moe_w1[i].astype(BF16),
                     moe_w3[i].astype(BF16), moe_w2[i].astype(BF16)).reshape(x.shape)
            x = _finish(x, y, mod, row(final_norm_g) if last else None)
    return x
```

```python
import functools

import jax
import jax.numpy as jnp
from jax import lax
from jax.experimental import pallas as pl
from jax.experimental.pallas import tpu as pltpu

F32 = jnp.float32
BF16 = jnp.bfloat16

POOL_WINDOWS = (2, 4, 8, 16)
POOL_GROUP_W = 64
POOL_W = 256
CONV_W = 256
CONV_K = 31
RWKV_HEAD = 64
RWKV_W = 512
RWKV_HEADS = 8
W_LORA = 32
A_LORA = 32
V_LORA = 16
G_LORA = 64
RWKV_COLS = 3 * RWKV_W + W_LORA + A_LORA + G_LORA
ZR_COLS = 1792
OFF_CONV = POOL_W
OFF_RWKV = OFF_CONV + 2 * CONV_W
OFF_GATE = OFF_RWKV + RWKV_COLS
N_EXPERTS = 8
RMS_EPS = 1e-6
LN_EPS = 1e-5
GN_EPS = 64e-5

LANES = 128
SUBLANES = 8
VMEM_LIMIT = 56 * 1024 * 1024
CHUNK = 64
HALO = 32


def _sigmoid(x):
    return 1.0 / (1.0 + jnp.exp(-x))


def _silu(x):
    return x * _sigmoid(x)


def _dot(a, b):
    return jnp.dot(a, b, preferred_element_type=F32)


def _ada_rms(x, mod, g):
    y = x * lax.rsqrt(jnp.mean(x * x, axis=-1, keepdims=True) + RMS_EPS) * g
    return y * (1.0 + mod[1:2, :]) + mod[0:1, :]


def _params(sem):
    return pltpu.CompilerParams(dimension_semantics=sem, vmem_limit_bytes=VMEM_LIMIT)


def _mod_kernel(c_ref, w_ref, b_ref, o_ref):
    c = c_ref[...]
    o_ref[0] = jnp.dot(_silu(c), w_ref[0], preferred_element_type=F32,
                       precision=lax.Precision.HIGHEST) + b_ref[0]


def _ada_mods(c, ada_w, ada_b):
    b_, d = c.shape
    n_sub = ada_w.shape[0] * ada_w.shape[1]
    w = ada_w.reshape(n_sub, d, 3 * d)
    bias = ada_b.reshape(n_sub, 1, 3 * d)
    c8 = jnp.zeros((SUBLANES, d), F32).at[:b_].set(c)
    tn = 1024
    out = pl.pallas_call(
        _mod_kernel,
        grid=(n_sub, 3 * d // tn),
        in_specs=[pl.BlockSpec((SUBLANES, d), lambda s, j: (0, 0)),
                  pl.BlockSpec((1, d, tn), lambda s, j: (s, 0, j)),
                  pl.BlockSpec((1, 1, tn), lambda s, j: (s, 0, j))],
        out_specs=pl.BlockSpec((1, SUBLANES, tn), lambda s, j: (s, 0, j)),
        out_shape=jax.ShapeDtypeStruct((n_sub, SUBLANES, 3 * d), F32),
        compiler_params=_params(("parallel", "parallel")),
        name="ada_mod",
    )(c8, w, bias)
    return out[:, :b_].reshape(n_sub, b_, 3, d)


def _inproj_kernel(x_ref, mod_ref, g_ref, wpc_ref, wr_ref, wg_ref, zpc_ref, zr_ref, zg_ref):
    hb = _ada_rms(x_ref[0], mod_ref[0], g_ref[...]).astype(BF16)
    zpc_ref[0] = _dot(hb, wpc_ref[...])
    zr_ref[0] = _dot(hb, wr_ref[...])
    zg_ref[0] = _sigmoid(_dot(hb, wg_ref[...])).astype(BF16)


def _in_proj(x, mod, g, wpc, wr, wg, ts=256):
    b_, s_, d = x.shape
    const = lambda b, i: (0, 0)
    tile = lambda b, i: (b, i, 0)
    return pl.pallas_call(
        _inproj_kernel,
        grid=(b_, s_ // ts),
        in_specs=[pl.BlockSpec((1, ts, d), tile),
                  pl.BlockSpec((1, 3, d), lambda b, i: (b, 0, 0)),
                  pl.BlockSpec((1, d), const),
                  pl.BlockSpec(wpc.shape, const),
                  pl.BlockSpec(wr.shape, const),
                  pl.BlockSpec(wg.shape, const)],
        out_specs=[pl.BlockSpec((1, ts, wpc.shape[1]), tile),
                   pl.BlockSpec((1, ts, wr.shape[1]), tile),
                   pl.BlockSpec((1, ts, wg.shape[1]), tile)],
        out_shape=[jax.ShapeDtypeStruct((b_, s_, wpc.shape[1]), F32),
                   jax.ShapeDtypeStruct((b_, s_, wr.shape[1]), F32),
                   jax.ShapeDtypeStruct((b_, s_, wg.shape[1]), BF16)],
        compiler_params=_params(("parallel", "parallel")),
        name="in_proj",
    )(x, mod, g, wpc, wr, wg)


def _head_sum(x):
    low = lax.broadcasted_iota(jnp.int32, (1, LANES), 1) < RWKV_HEAD
    outs = []
    for cb in range(x.shape[1] // LANES):
        xc = x[:, cb * LANES:(cb + 1) * LANES]
        s_lo = jnp.sum(jnp.where(low, xc, 0.0), axis=-1, keepdims=True)
        s_hi = jnp.sum(jnp.where(low, 0.0, xc), axis=-1, keepdims=True)
        outs.append(jnp.where(low, s_lo, s_hi))
    return jnp.concatenate(outs, axis=-1)


def _prep_kernel(has_vres, z_ref, zprev_ref, mu_ref, w0_ref, w2_ref, a0_ref, a2_ref, g2_ref,
                 ksc_ref, ka_ref, rk_ref, *rest):
    if has_vres:
        vf_ref, v0_ref, v2_ref = rest[:3]
        rest = rest[3:]
    r_ref, lw_ref, k_ref, v_ref, kk_ref, b_ref, g_ref, bonus_ref = rest

    i = pl.program_id(1)
    z = z_ref[0]
    ts = z.shape[0]
    prev = jnp.where(i > 0, zprev_ref[0, SUBLANES - 1:SUBLANES, :], 0.0)
    row = lax.broadcasted_iota(jnp.int32, (ts, 1), 0)
    zsh = jnp.where(row == 0, prev, pltpu.roll(z, shift=1, axis=0))
    z = z + (zsh - z) * mu_ref[...]

    w3 = RWKV_W
    r = z[:, 0:w3]
    k = z[:, w3:2 * w3]
    v = z[:, 2 * w3:3 * w3]
    o = 3 * w3
    wd = z[:, o:o + W_LORA]
    ad = z[:, o + W_LORA:o + W_LORA + A_LORA]
    gd = z[:, o + W_LORA + A_LORA:o + W_LORA + A_LORA + G_LORA]

    wpre = w0_ref[...] + _dot(jnp.tanh(wd).astype(BF16), w2_ref[...])
    y = -wpre
    softplus = jnp.maximum(y, 0.0) + jnp.log(1.0 + jnp.exp(-jnp.abs(y)))
    w = -softplus - 0.5
    lw_ref[0] = -jnp.exp(w)
    a = _sigmoid(a0_ref[...] + _dot(ad.astype(BF16), a2_ref[...]))
    g_ref[0] = _dot(_sigmoid(gd).astype(BF16), g2_ref[...])
    if has_vres:
        vd = z[:, RWKV_COLS:RWKV_COLS + V_LORA]
        mix = _sigmoid(v0_ref[...] + _dot(vd.astype(BF16), v2_ref[...]))
        v = v + (vf_ref[0] - v) * mix
    kk = k * ksc_ref[...]
    kk = kk / jnp.maximum(jnp.sqrt(_head_sum(kk * kk)), 1e-12)
    kh = k * (1.0 + (a - 1.0) * ka_ref[...])
    r_ref[0] = r
    k_ref[0] = kh
    v_ref[0] = v
    kk_ref[0] = kk
    b_ref[0] = kk * a
    bonus_ref[0] = _head_sum(r * kh * rk_ref[...]) * v


def _rwkv_prep(zr, mu, w0, w2, a0, a2, g2, ksc, ka, rk, vres, ts=256):
    b_, s_, zc = zr.shape
    has_vres = vres is not None
    const = lambda b, i: (0, 0)
    tile = lambda b, i: (b, i, 0)
    vec = pl.BlockSpec((1, RWKV_W), const)
    in_specs = [pl.BlockSpec((1, ts, zc), tile),
                pl.BlockSpec((1, SUBLANES, zc),
                             lambda b, i: (b, jnp.maximum(i * (ts // SUBLANES) - 1, 0), 0)),
                pl.BlockSpec((1, zc), const),
                vec, pl.BlockSpec((W_LORA, RWKV_W), const),
                vec, pl.BlockSpec((A_LORA, RWKV_W), const),
                pl.BlockSpec((G_LORA, RWKV_W), const),
                vec, vec, vec]
    args = [zr, zr, mu, w0, w2, a0, a2, g2, ksc, ka, rk]
    if has_vres:
        v_first, v0, v2 = vres
        in_specs += [pl.BlockSpec((1, ts, RWKV_W), tile), vec,
                     pl.BlockSpec((V_LORA, RWKV_W), const)]
        args += [v_first, v0, v2]
    out = jax.ShapeDtypeStruct((b_, s_, RWKV_W), F32)
    return pl.pallas_call(
        functools.partial(_prep_kernel, has_vres),
        grid=(b_, s_ // ts),
        in_specs=in_specs,
        out_specs=[pl.BlockSpec((1, ts, RWKV_W), tile)] * 8,
        out_shape=[out] * 8,
        compiler_params=_params(("parallel", "parallel")),
        name="rwkv_prep",
    )(*args)


def _split3(x):
    hi = x.astype(BF16)
    r1 = x - hi.astype(F32)
    mid = r1.astype(BF16)
    lo = (r1 - mid.astype(F32)).astype(BF16)
    return hi, mid, lo


def _dot_nt(a, b):
    return lax.dot_general(a, b, (((1,), (1,)), ((), ())), preferred_element_type=F32)


def _dot_tn(a, b):
    return lax.dot_general(a, b, (((0,), (0,)), ((), ())), preferred_element_type=F32)


CHUNKS_PER_STEP = 4


def _wkv_kernel(r_ref, lw_ref, k_ref, v_ref, kk_ref, b_ref, o_ref, z_ref):
    @pl.when(pl.program_id(1) == 0)
    def _():
        z_ref[...] = jnp.zeros_like(z_ref)

    c_ = CHUNK
    n_ = RWKV_HEAD
    nc = CHUNKS_PER_STEP
    n_pairs = RWKV_HEADS // 2
    n_steps = r_ref.shape[1] // (c_ * nc)
    ri = lax.broadcasted_iota(jnp.int32, (c_, c_), 0)
    ci = lax.broadcasted_iota(jnp.int32, (c_, c_), 1)
    tril_incl = (ri >= ci)
    tril_strict = (ri > ci)
    rs = lax.broadcasted_iota(jnp.int32, (nc * c_, nc * c_), 0)
    cs = lax.broadcasted_iota(jnp.int32, (nc * c_, nc * c_), 1)
    tri_b = jnp.where((rs >= cs) & ((rs // c_) == (cs // c_)), 1.0, 0.0).astype(BF16)
    rl = lax.broadcasted_iota(jnp.int32, (LANES, LANES), 0)
    cl = lax.broadcasted_iota(jnp.int32, (LANES, LANES), 1)
    eye_pair = (rl == cl)
    same_head = ((rl < n_) == (cl < n_))
    head0 = lax.broadcasted_iota(jnp.int32, (1, LANES), 1) < n_
    pl_ = lambda a, p: a[:, p * LANES:(p + 1) * LANES]
    ch_ = lambda a, q: a[q * c_:(q + 1) * c_]

    def step_body(st, carry):
        rows = pl.ds(pl.multiple_of(st * (nc * c_), nc * c_), nc * c_)
        lw = lw_ref[0, rows, :]
        hi, mid, lo = _split3(lw)
        cum = _dot(tri_b, hi) + _dot(tri_b, mid) + _dot(tri_b, lo)
        cum_last = jnp.concatenate(
            [jnp.broadcast_to(cum[(q + 1) * c_ - 1:(q + 1) * c_, :], (c_, cum.shape[1])) for q in range(nc)],
            axis=0)
        w_inv = jnp.exp(-cum)
        w_tail = jnp.exp(cum_last - cum)
        w_chunk = jnp.exp(cum_last)
        kk = kk_ref[0, rows, :]
        bb = b_ref[0, rows, :]
        kx = k_ref[0, rows, :]
        at32 = -kk * jnp.exp(cum - lw)
        rt32 = r_ref[0, rows, :] * jnp.exp(cum)
        at = at32.astype(BF16)
        rt = rt32.astype(BF16)
        bt = (bb * w_inv).astype(BF16)
        kt = (kx * w_inv).astype(BF16)
        bh = (bb * w_tail).astype(BF16)
        kh = (kx * w_tail).astype(BF16)
        v32 = v_ref[0, rows, :]
        vrot = jnp.concatenate([pltpu.roll(pl_(v32, p), shift=n_, axis=1) for p in range(n_pairs)],
                               axis=1).astype(BF16)

        chains = [(q, p, h) for q in range(nc) for p in range(n_pairs) for h in range(2)]
        aa = []
        for q, p, h in chains:
            lhs = jnp.concatenate([ch_(pl_(at, p), q), ch_(pl_(rt, p), q)], axis=0)
            rhs = jnp.concatenate([ch_(pl_(bt, p), q), ch_(pl_(kt, p), q)], axis=0)
            keep = head0 if h == 0 else jnp.logical_not(head0)
            aa.append(_dot_nt(jnp.where(keep, lhs, jnp.zeros_like(lhs)), rhs))
        a_ab = [jnp.where(tril_strict, a[:c_, :c_], 0.0).astype(BF16) for a in aa]
        a_rb = [jnp.where(tril_incl, a[c_:, :c_], 0.0).astype(BF16) for a in aa]
        a_k = [jnp.concatenate([jnp.where(tril_strict, a[:c_, c_:], 0.0),
                                jnp.where(tril_incl, a[c_:, c_:], 0.0)], axis=0).astype(BF16)
               for a in aa]
        av = [_dot(a_k[i], ch_(pl_(vrot, p), q)) for i, (q, p, h) in enumerate(chains)]
        own = [head0 if h == 0 else jnp.logical_not(head0) for q, p, h in chains]
        xs = [jnp.where(own[i], ch_(pl_(at32, p), q), av[i][:c_])
              for i, (q, p, h) in enumerate(chains)]
        ps = a_ab
        for it in range(6):
            xs = [x + _dot(pw, x.astype(BF16)) for x, pw in zip(xs, ps)]
            if it < 5:
                ps = [_dot(pw, pw).astype(BF16) for pw in ps]
        qos = [jnp.where(own[i], ch_(pl_(rt32, p), q), av[i][c_:]) + _dot(a_rb[i], xs[i].astype(BF16))
               for i, (q, p, h) in enumerate(chains)]
        ops = []
        for q in range(nc):
            for p in range(n_pairs):
                i0 = (q * n_pairs + p) * 2
                g_pair = jnp.where(head0, xs[i0], xs[i0 + 1])
                u_swap = jnp.where(head0, xs[i0 + 1], xs[i0])
                q_pair = jnp.where(head0, qos[i0], qos[i0 + 1])
                o_swap = jnp.where(head0, qos[i0 + 1], qos[i0])
                gu = jnp.concatenate([g_pair, u_swap], axis=1).astype(BF16)
                m1 = _dot_tn(ch_(pl_(bh, p), q), gu)
                m2 = _dot_tn(ch_(pl_(kh, p), q), ch_(pl_(vrot, p), q))
                wc = ch_(pl_(w_chunk, p), q)[0:1, :]
                phi_t = jnp.where(eye_pair, wc, 0.0) + jnp.where(same_head, m1[:, :LANES], 0.0)
                psi_t = jnp.where(same_head, 0.0, m1[:, LANES:] + m2)
                ops.append((q_pair.astype(BF16), o_swap, phi_t.astype(BF16), psi_t))
        for p in range(n_pairs):
            z = z_ref[p]
            outs = []
            for q in range(nc):
                q_pair, o_swap, phi_t, psi_t = ops[q * n_pairs + p]
                zb = z.astype(BF16)
                outs.append(pltpu.roll(_dot(q_pair, zb) + o_swap, shift=n_, axis=1))
                z = _dot(phi_t, zb) + psi_t
            z_ref[p] = z
            o_ref[0, rows, p * LANES:(p + 1) * LANES] = jnp.concatenate(outs, axis=0)
        return carry

    lax.fori_loop(0, n_steps, step_body, 0)


def _wkv_scan(r, lw, k, v, kk, bb, tc=512):
    b_, s_, w_ = r.shape
    tile = lambda b, i: (b, i, 0)
    spec = pl.BlockSpec((1, tc, w_), tile)
    return pl.pallas_call(
        _wkv_kernel,
        grid=(b_, s_ // tc),
        in_specs=[spec] * 6,
        out_specs=spec,
        out_shape=jax.ShapeDtypeStruct((b_, s_, w_), F32),
        scratch_shapes=[pltpu.VMEM((RWKV_HEADS // 2, LANES, LANES), F32)],
        compiler_params=_params(("parallel", "arbitrary")),
        name="wkv_scan",
    )(r, lw, k, v, kk, bb)


def _post_kernel(x_ref, mod_ref, zpc_ref, halo_ref, zg_ref, o_ref, bonus_ref, g_ref,
                 poolw_ref, pscale_ref, pproj_ref, convw_ref, convb_ref, lng_ref, lnb_ref,
                 cproj_ref, lnxg_ref, lnxb_ref, rproj_ref, wo_ref, out_ref, ext_ref):
    i = pl.program_id(1)
    ts = x_ref.shape[1]
    halo = jnp.where(i > 0, halo_ref[0], 0.0)
    cur = zpc_ref[0]

    up = cur[:, :POOL_W]
    ext = jnp.concatenate([halo[:, :POOL_W], up], axis=0)
    e1 = ext + pltpu.roll(ext, shift=1, axis=0)
    e2 = e1 + pltpu.roll(e1, shift=2, axis=0)
    e3 = e2 + pltpu.roll(e2, shift=4, axis=0)
    e4 = e3 + pltpu.roll(e3, shift=8, axis=0)
    lane = lax.broadcasted_iota(jnp.int32, (1, POOL_W), 1)
    grp = lane // POOL_GROUP_W
    pooled = jnp.where(grp == 0, e1, jnp.where(grp == 1, e2, jnp.where(grp == 2, e3, e4)))[HALO:]
    win = jnp.where(grp == 0, 2.0, jnp.where(grp == 1, 4.0, jnp.where(grp == 2, 8.0, 16.0)))
    pos = (i * ts + 1 + lax.broadcasted_iota(jnp.int32, (ts, 1), 0)).astype(F32)
    p = pooled / jnp.minimum(pos, win) - up
    p = _dot(p.astype(BF16), poolw_ref[...]) * pscale_ref[...]
    y_pool = _dot(p.astype(BF16), pproj_ref[...])

    za = jnp.concatenate([halo[:, POOL_W:POOL_W + CONV_W], cur[:, POOL_W:POOL_W + CONV_W]], axis=0)
    zb = jnp.concatenate([halo[:, POOL_W + CONV_W:], cur[:, POOL_W + CONV_W:]], axis=0)
    ext_ref[...] = za * _sigmoid(zb)
    acc = jnp.zeros((ts, CONV_W), F32) + convb_ref[...]
    for j in range(CONV_K):
        st = HALO - (CONV_K - 1) + j
        acc = acc + ext_ref[st:st + ts, :] * convw_ref[j:j + 1, :]
    mean = jnp.mean(acc, axis=-1, keepdims=True)
    dev = acc - mean
    var = jnp.mean(dev * dev, axis=-1, keepdims=True)
    un = dev * lax.rsqrt(var + LN_EPS) * lng_ref[...] + lnb_ref[...]
    y_conv = _dot(_silu(un).astype(BF16), cproj_ref[...])

    o = o_ref[0]
    inv_n = 1.0 / RWKV_HEAD
    mu = _head_sum(o) * inv_n
    od = o - mu
    ovar = _head_sum(od * od) * inv_n
    on = od * lax.rsqrt(ovar + GN_EPS) * lnxg_ref[...] + lnxb_ref[...]
    y_rwkv = _dot(((on + bonus_ref[0]) * g_ref[0]).astype(BF16), rproj_ref[...])

    d = x_ref.shape[2]
    zg = zg_ref[0]
    merged = (zg[:, :d].astype(F32) * y_pool + zg[:, d:2 * d].astype(F32) * y_conv
              + zg[:, 2 * d:].astype(F32) * y_rwkv)
    out_ref[0] = x_ref[0] + mod_ref[0, 2:3, :] * _dot(merged.astype(BF16), wo_ref[...])


def _post(x, mod, zpc, zg, o, bonus, g, poolw, pscale, pproj, convw, convb, lng, lnb, cproj,
          lnxg, lnxb, rproj, wo, ts=256):
    b_, s_, d = x.shape
    const = lambda b, i: (0, 0)
    tile = lambda b, i: (b, i, 0)
    full = lambda a: pl.BlockSpec(a.shape, const)
    in_specs = [pl.BlockSpec((1, ts, d), tile),
                pl.BlockSpec((1, 3, d), lambda b, i: (b, 0, 0)),
                pl.BlockSpec((1, ts, zpc.shape[2]), tile),
                pl.BlockSpec((1, HALO, zpc.shape[2]),
                             lambda b, i: (b, jnp.maximum(i * (ts // HALO) - 1, 0), 0)),
                pl.BlockSpec((1, ts, zg.shape[2]), tile),
                pl.BlockSpec((1, ts, RWKV_W), tile),
                pl.BlockSpec((1, ts, RWKV_W), tile),
                pl.BlockSpec((1, ts, RWKV_W), tile)]
    weights = [poolw, pscale, pproj, convw, convb, lng, lnb, cproj, lnxg, lnxb, rproj, wo]
    in_specs += [full(a) for a in weights]
    return pl.pallas_call(
        _post_kernel,
        grid=(b_, s_ // ts),
        in_specs=in_specs,
        out_specs=pl.BlockSpec((1, ts, d), tile),
        out_shape=jax.ShapeDtypeStruct((b_, s_, d), F32),
        scratch_shapes=[pltpu.VMEM((HALO + ts, CONV_W), F32)],
        compiler_params=_params(("parallel", "parallel")),
        name="mix_merge",
    )(x, mod, zpc, zpc, zg, o, bonus, g, *weights)


def _router_kernel(x_ref, mod_ref, g_ref, wr_ref, gate_ref, h_ref):
    h = _ada_rms(x_ref[0], mod_ref[0], g_ref[...])
    h_ref[0] = h.astype(BF16)
    logits = jnp.dot(h, wr_ref[...], preferred_element_type=F32, precision=lax.Precision.HIGHEST)
    n_e = logits.shape[1]
    lane = lax.broadcasted_iota(jnp.int32, logits.shape, 1)
    m1 = jnp.max(logits, axis=-1, keepdims=True)
    i1 = jnp.min(jnp.where(logits == m1, lane, n_e), axis=-1, keepdims=True)
    sel1 = lane == i1
    rest = jnp.where(sel1, -jnp.inf, logits)
    m2 = jnp.max(rest, axis=-1, keepdims=True)
    i2 = jnp.min(jnp.where(rest == m2, lane, n_e), axis=-1, keepdims=True)
    sel2 = lane == i2
    e2 = jnp.exp(m2 - m1)
    w1 = 1.0 / (1.0 + e2)
    gate_ref[0] = jnp.where(sel1, w1, 0.0) + jnp.where(sel2, e2 * w1, 0.0)


def _router(x, mod, g, wr, ts=512):
    b_, s_, d = x.shape
    n_e = wr.shape[1]
    tile = lambda b, i: (b, i, 0)
    return pl.pallas_call(
        _router_kernel,
        grid=(b_, s_ // ts),
        in_specs=[pl.BlockSpec((1, ts, d), tile),
                  pl.BlockSpec((1, 3, d), lambda b, i: (b, 0, 0)),
                  pl.BlockSpec((1, d), lambda b, i: (0, 0)),
                  pl.BlockSpec(wr.shape, lambda b, i: (0, 0))],
        out_specs=[pl.BlockSpec((1, ts, n_e), tile), pl.BlockSpec((1, ts, d), tile)],
        out_shape=[jax.ShapeDtypeStruct((b_, s_, n_e), F32), jax.ShapeDtypeStruct((b_, s_, d), BF16)],
        compiler_params=_params(("parallel", "parallel")),
        name="moe_router",
    )(x, mod, g, wr)


PIECE = 128
SUB = 256


def _moe_kernel(cnt_ref, lo_ref, hi_ref, h_ref, rank_em_ref, rank_tm_ref, gate_ref,
                w1_ref, w3_ref, w2_ref, y_ref, xs_ref, acc_ref, tmp_ref):
    b = pl.program_id(0)
    e = pl.program_id(1)
    f = pl.program_id(2)
    n_e = pl.num_programs(1)
    tb = h_ref.shape[0]
    max_pieces = tb // PIECE
    pair = b * n_e + e
    n_pieces = (cnt_ref[pair] + (PIECE - 1)) // PIECE
    piece_rows = lambda j: pl.ds(pl.multiple_of(j * PIECE, PIECE), PIECE)
    sub_rows = lambda s: pl.ds(pl.multiple_of(s * SUB, SUB), SUB)

    @pl.when((e == 0) & (f == 0))
    def _():
        y_ref[...] = jnp.zeros_like(y_ref)

    @pl.when(f == 0)
    def _():
        slot = lax.broadcasted_iota(jnp.int32, (PIECE, 1), 0)

        def gather(j, carry):
            tmp_ref[...] = jnp.zeros_like(tmp_ref)

            def window(s, c2):
                onehot = jnp.where(rank_em_ref[0, 0, s] == slot + j * PIECE, 1.0, 0.0).astype(BF16)
                tmp_ref[...] += _dot(onehot, h_ref[sub_rows(s), :])
                return c2
            lax.fori_loop(lo_ref[pair * max_pieces + j], hi_ref[pair * max_pieces + j] + 1, window, 0)
            xs_ref[piece_rows(j), :] = tmp_ref[...].astype(BF16)
            return carry
        lax.fori_loop(0, n_pieces, gather, 0)

    def expert(j, carry):
        rows = piece_rows(j)
        xj = xs_ref[rows, :]
        act = _silu(_dot(xj, w1_ref[0])) * _dot(xj, w3_ref[0])
        part = _dot(act.astype(BF16), w2_ref[0])

        @pl.when(f == 0)
        def _():
            acc_ref[rows, :] = part

        @pl.when(f > 0)
        def _():
            acc_ref[rows, :] += part
        return carry
    lax.fori_loop(0, n_pieces, expert, 0)

    @pl.when(f == pl.num_programs(2) - 1)
    def _():
        lane_e = lax.broadcasted_iota(jnp.int32, (1, n_e), 1) == e
        slot = lax.broadcasted_iota(jnp.int32, (1, PIECE), 1)

        def scatter(j, carry):
            yj = acc_ref[piece_rows(j), :].astype(BF16)

            def window(s, c2):
                rows = sub_rows(s)
                rank_col = jnp.sum(jnp.where(lane_e, rank_tm_ref[rows, :], 0), axis=-1, keepdims=True)
                gate_col = jnp.sum(jnp.where(lane_e, gate_ref[rows, :], 0.0), axis=-1, keepdims=True)
                onehot_t = jnp.where(rank_col == slot + j * PIECE, 1.0, 0.0).astype(BF16)
                y_ref[rows, :] += gate_col * _dot(onehot_t, yj)
                return c2
            lax.fori_loop(lo_ref[pair * max_pieces + j], hi_ref[pair * max_pieces + j] + 1, window, 0)
            return carry
        lax.fori_loop(0, n_pieces, scatter, 0)


def _moe(h, gate, w1, w3, w2, tb=2048, tf=896):
    t_, d = h.shape
    n_e, _, ff = w1.shape
    nb = t_ // tb
    n_sub = tb // SUB
    max_pieces = tb // PIECE
    sel = (gate != 0.0).reshape(nb, tb, n_e).astype(jnp.int32)
    rank_tm = jnp.where(sel > 0, jnp.cumsum(sel, axis=1) - 1, -1)
    ends = jnp.cumsum(jnp.sum(sel.reshape(nb, n_sub, SUB, n_e), axis=2), axis=1)
    ends = jnp.transpose(ends, (0, 2, 1))
    cnt = ends[:, :, -1]
    starts = jnp.concatenate([jnp.zeros_like(ends[:, :, :1]), ends[:, :, :-1]], axis=2)
    first = (jnp.arange(max_pieces, dtype=jnp.int32) * PIECE)[None, None, :, None]
    last = jnp.minimum(first + PIECE, cnt[:, :, None, None])
    win_lo = jnp.sum((ends[:, :, None, :] <= first).astype(jnp.int32), axis=-1)
    win_hi = jnp.sum((starts[:, :, None, :] < last).astype(jnp.int32), axis=-1) - 1
    rank_em = jnp.transpose(rank_tm, (0, 2, 1)).reshape(nb, n_e, n_sub, 1, SUB)
    grid_spec = pltpu.PrefetchScalarGridSpec(
        num_scalar_prefetch=3,
        grid=(nb, n_e, ff // tf),
        in_specs=[pl.BlockSpec((tb, d), lambda b, e, f, *_: (b, 0)),
                  pl.BlockSpec((1, 1, n_sub, 1, SUB), lambda b, e, f, *_: (b, e, 0, 0, 0)),
                  pl.BlockSpec((tb, n_e), lambda b, e, f, *_: (b, 0)),
                  pl.BlockSpec((tb, n_e), lambda b, e, f, *_: (b, 0)),
                  pl.BlockSpec((1, d, tf), lambda b, e, f, *_: (e, 0, f)),
                  pl.BlockSpec((1, d, tf), lambda b, e, f, *_: (e, 0, f)),
                  pl.BlockSpec((1, tf, d), lambda b, e, f, *_: (e, f, 0))],
        out_specs=pl.BlockSpec((tb, d), lambda b, e, f, *_: (b, 0)),
        scratch_shapes=[pltpu.VMEM((tb, d), BF16), pltpu.VMEM((tb, d), F32),
                        pltpu.VMEM((PIECE, d), F32)])
    return pl.pallas_call(
        _moe_kernel,
        grid_spec=grid_spec,
        out_shape=jax.ShapeDtypeStruct((t_, d), F32),
        compiler_params=_params(("parallel", "arbitrary", "arbitrary")),
        name="moe_sparse",
    )(cnt.reshape(-1), win_lo.reshape(-1), win_hi.reshape(-1), h, rank_em,
      rank_tm.reshape(t_, n_e), gate, w1, w3, w2)


def _ffn_kernel(x_ref, mod_ref, g_ref, w1_ref, w3_ref, w2_ref, out_ref, h_ref, acc_ref):
    f = pl.program_id(2)

    @pl.when(f == 0)
    def _():
        h_ref[...] = _ada_rms(x_ref[0], mod_ref[0], g_ref[...]).astype(BF16)
        acc_ref[...] = jnp.zeros_like(acc_ref)

    h = h_ref[...]
    act = _silu(_dot(h, w1_ref[...])) * _dot(h, w3_ref[...])
    acc_ref[...] += _dot(act.astype(BF16), w2_ref[...])

    @pl.when(f == pl.num_programs(2) - 1)
    def _():
        out_ref[0] = x_ref[0] + mod_ref[0, 2:3, :] * acc_ref[...]


def _ffn(x, mod, g, w1, w3, w2, tm=512, tf=1408):
    b_, s_, d = x.shape
    ff = w1.shape[1]
    tile = lambda b, i, f: (b, i, 0)
    return pl.pallas_call(
        _ffn_kernel,
        grid=(b_, s_ // tm, ff // tf),
        in_specs=[pl.BlockSpec((1, tm, d), tile),
                  pl.BlockSpec((1, 3, d), lambda b, i, f: (b, 0, 0)),
                  pl.BlockSpec((1, d), lambda b, i, f: (0, 0)),
                  pl.BlockSpec((d, tf), lambda b, i, f: (0, f)),
                  pl.BlockSpec((d, tf), lambda b, i, f: (0, f)),
                  pl.BlockSpec((tf, d), lambda b, i, f: (f, 0))],
        out_specs=pl.BlockSpec((1, tm, d), tile),
        out_shape=jax.ShapeDtypeStruct((b_, s_, d), F32),
        scratch_shapes=[pltpu.VMEM((tm, d), BF16), pltpu.VMEM((tm, d), F32)],
        compiler_params=_params(("parallel", "parallel", "arbitrary")),
        name="ffn",
    )(x, mod, g, w1, w3, w2)


def _finish_kernel(has_y, has_norm, x_ref, *rest):
    x = x_ref[0]
    if has_y:
        y_ref, mod_ref = rest[:2]
        rest = rest[2:]
        x = x + mod_ref[0, 2:3, :] * y_ref[0]
    if has_norm:
        g_ref = rest[0]
        rest = rest[1:]
        x = x * lax.rsqrt(jnp.mean(x * x, axis=-1, keepdims=True) + RMS_EPS) * g_ref[...]
    rest[0][0] = x


def _finish(x, y=None, mod=None, g=None, ts=512):
    b_, s_, d = x.shape
    tile = lambda b, i: (b, i, 0)
    in_specs = [pl.BlockSpec((1, ts, d), tile)]
    args = [x]
    if y is not None:
        in_specs += [pl.BlockSpec((1, ts, d), tile), pl.BlockSpec((1, 3, d), lambda b, i: (b, 0, 0))]
        args += [y, mod]
    if g is not None:
        in_specs.append(pl.BlockSpec((1, d), lambda b, i: (0, 0)))
        args.append(g)
    return pl.pallas_call(
        functools.partial(_finish_kernel, y is not None, g is not None),
        grid=(b_, s_ // ts),
        in_specs=in_specs,
        out_specs=pl.BlockSpec((1, ts, d), tile),
        out_shape=jax.ShapeDtypeStruct((b_, s_, d), F32),
        compiler_params=_params(("parallel", "parallel")),
        name="finish",
    )(*args)


def _block_diag(w):
    g_, c_, _ = w.shape
    out = jnp.zeros((g_ * c_, g_ * c_), w.dtype)
    for gi in range(g_):
        out = out.at[gi * c_:(gi + 1) * c_, gi * c_:(gi + 1) * c_].set(w[gi])
    return out


def kernel(x, c, ada_w, ada_b, norm_g, w_in, pool_w, pool_scale, pool_proj, conv_w, conv_b, conv_ln_g, conv_ln_b, conv_proj, rwkv_mu, rwkv_w0, rwkv_w2, rwkv_a0, rwkv_a2, rwkv_g2, rwkv_kk_scale, rwkv_ka, rwkv_rk, rwkv_lnx_g, rwkv_lnx_b, rwkv_proj, vres_w_down, vres_mu, vres_v0, vres_v2, w_o, ffn_w1, ffn_w3, ffn_w2, moe_router, moe_w1, moe_w3, moe_w2, final_norm_g):
    depth = w_in.shape[0]
    d = x.shape[2]
    row = lambda a: a.reshape(1, -1)
    mods = _ada_mods(c, ada_w, ada_b)
    v_first = None
    for l in range(depth):
        mod = mods[2 * l]
        zr_pad = jnp.zeros((d, ZR_COLS - RWKV_COLS - V_LORA), F32)
        mu_pad = jnp.zeros((ZR_COLS - RWKV_COLS - V_LORA,), F32)
        if l == 0:
            w_vd = jnp.zeros((d, V_LORA), F32)
            mu_vd = jnp.zeros((V_LORA,), F32)
        else:
            w_vd = vres_w_down[l - 1]
            mu_vd = vres_mu[l - 1]
        wr = jnp.concatenate([w_in[l][:, OFF_RWKV:OFF_GATE], w_vd, zr_pad], axis=1).astype(BF16)
        mu = jnp.concatenate([rwkv_mu[l], mu_vd, mu_pad]).reshape(1, ZR_COLS)
        wpc = w_in[l][:, :OFF_RWKV].astype(BF16)
        wg = w_in[l][:, OFF_GATE:OFF_GATE + 3 * d].astype(BF16)
        zpc, zr, zg = _in_proj(x, mod, row(norm_g[l, 0]), wpc, wr, wg)
        vres = None
        if l > 0:
            vres = (v_first, row(vres_v0[l - 1]), vres_v2[l - 1].astype(BF16))
        r, lw, kh, v, kk, bb, g, bonus = _rwkv_prep(
            zr, mu, row(rwkv_w0[l]), rwkv_w2[l].astype(BF16), row(rwkv_a0[l]),
            rwkv_a2[l].astype(BF16), rwkv_g2[l].astype(BF16), row(rwkv_kk_scale[l]),
            row(rwkv_ka[l]), row(rwkv_rk[l]), vres)
        if l == 0:
            v_first = v
        o = _wkv_scan(r, lw, kh, v, kk, bb)
        x = _post(x, mod, zpc, zg, o, bonus, g,
                  _block_diag(pool_w[l]).astype(BF16), row(pool_scale[l]), pool_proj[l].astype(BF16),
                  conv_w[l], row(conv_b[l]), row(conv_ln_g[l]), row(conv_ln_b[l]),
                  conv_proj[l].astype(BF16), row(rwkv_lnx_g[l]), row(rwkv_lnx_b[l]),
                  rwkv_proj[l].astype(BF16), w_o[l].astype(BF16))
        mod = mods[2 * l + 1]
        gn = row(norm_g[l, 1])
        i = l // 2
        last = l == depth - 1
        if l % 2 == 0:
            x = _ffn(x, mod, gn, ffn_w1[i].astype(BF16), ffn_w3[i].astype(BF16), ffn_w2[i].astype(BF16))
            if last:
                x = _finish(x, g=row(final_norm_g))
        else:
            gate, h = _router(x, mod, gn, moe_router[i])
            n_e = gate.shape[2]
            y = _moe(h.reshape(-1, d), gate.reshape(-1, n_e), moe_w1[i].astype(BF16),
                     moe_w3[i].astype(BF16), moe_w2[i].astype(BF16)).reshape(x.shape)
            x = _finish(x, y, mod, row(final_norm_g) if last else None)
    return x
```

```python
import functools

import jax
import jax.numpy as jnp
from jax import lax
from jax.experimental import pallas as pl
from jax.experimental.pallas import tpu as pltpu

F32 = jnp.float32
BF16 = jnp.bfloat16

POOL_WINDOWS = (2, 4, 8, 16)
POOL_GROUP_W = 64
POOL_W = 256
CONV_W = 256
CONV_K = 31
RWKV_HEAD = 64
RWKV_W = 512
RWKV_HEADS = 8
W_LORA = 32
A_LORA = 32
V_LORA = 16
G_LORA = 64
RWKV_COLS = 3 * RWKV_W + W_LORA + A_LORA + G_LORA
ZR_COLS = 1792
OFF_CONV = POOL_W
OFF_RWKV = OFF_CONV + 2 * CONV_W
OFF_GATE = OFF_RWKV + RWKV_COLS
N_EXPERTS = 8
RMS_EPS = 1e-6
LN_EPS = 1e-5
GN_EPS = 64e-5

LANES = 128
SUBLANES = 8
VMEM_LIMIT = 56 * 1024 * 1024
CHUNK = 64
HALO = 32


def _sigmoid(x):
    return 1.0 / (1.0 + jnp.exp(-x))


def _silu(x):
    return x * _sigmoid(x)


def _dot(a, b):
    return jnp.dot(a, b, preferred_element_type=F32)


def _ada_rms(x, mod, g):
    y = x * lax.rsqrt(jnp.mean(x * x, axis=-1, keepdims=True) + RMS_EPS) * g
    return y * (1.0 + mod[1:2, :]) + mod[0:1, :]


def _params(sem):
    return pltpu.CompilerParams(dimension_semantics=sem, vmem_limit_bytes=VMEM_LIMIT)


def _mod_kernel(c_ref, w_ref, b_ref, o_ref):
    c = c_ref[...]
    o_ref[0] = jnp.dot(_silu(c), w_ref[0], preferred_element_type=F32,
                       precision=lax.Precision.HIGHEST) + b_ref[0]


def _ada_mods(c, ada_w, ada_b):
    b_, d = c.shape
    n_sub = ada_w.shape[0] * ada_w.shape[1]
    w = ada_w.reshape(n_sub, d, 3 * d)
    bias = ada_b.reshape(n_sub, 1, 3 * d)
    c8 = jnp.zeros((SUBLANES, d), F32).at[:b_].set(c)
    tn = 1024
    out = pl.pallas_call(
        _mod_kernel,
        grid=(n_sub, 3 * d // tn),
        in_specs=[pl.BlockSpec((SUBLANES, d), lambda s, j: (0, 0)),
                  pl.BlockSpec((1, d, tn), lambda s, j: (s, 0, j)),
                  pl.BlockSpec((1, 1, tn), lambda s, j: (s, 0, j))],
        out_specs=pl.BlockSpec((1, SUBLANES, tn), lambda s, j: (s, 0, j)),
        out_shape=jax.ShapeDtypeStruct((n_sub, SUBLANES, 3 * d), F32),
        compiler_params=_params(("parallel", "parallel")),
        name="ada_mod",
    )(c8, w, bias)
    return out[:, :b_].reshape(n_sub, b_, 3, d)


def _head_sum(x):
    low = lax.broadcasted_iota(jnp.int32, (1, LANES), 1) < RWKV_HEAD
    outs = []
    for cb in range(x.shape[1] // LANES):
        xc = x[:, cb * LANES:(cb + 1) * LANES]
        s_lo = jnp.sum(jnp.where(low, xc, 0.0), axis=-1, keepdims=True)
        s_hi = jnp.sum(jnp.where(low, 0.0, xc), axis=-1, keepdims=True)
        outs.append(jnp.where(low, s_lo, s_hi))
    return jnp.concatenate(outs, axis=-1)


def _inproj_kernel(has_vres, x_ref, mod_ref, gn_ref, wpc_ref, wr_ref, wg_ref, mu_ref, w0_ref, w2_ref,
                   a0_ref, a2_ref, g2_ref, ksc_ref, ka_ref, rk_ref, *rest):
    if has_vres:
        vf_ref, v0_ref, v2_ref = rest[:3]
        rest = rest[3:]
    zpc_ref, zg_ref, r_ref, lw_ref, k_ref, v_ref, kk_ref, b_ref, g_ref, bonus_ref, last_ref = rest

    i = pl.program_id(1)
    hb = _ada_rms(x_ref[0], mod_ref[0], gn_ref[...]).astype(BF16)
    zpc_ref[0] = _dot(hb, wpc_ref[...])
    zg_ref[0] = _sigmoid(_dot(hb, wg_ref[...])).astype(BF16)
    z = _dot(hb, wr_ref[...])
    ts = z.shape[0]
    prev = jnp.where(i > 0, last_ref[SUBLANES - 1:SUBLANES, :], 0.0)
    last_ref[...] = z[ts - SUBLANES:, :]
    row = lax.broadcasted_iota(jnp.int32, (ts, 1), 0)
    zsh = jnp.where(row == 0, prev, pltpu.roll(z, shift=1, axis=0))
    z = z + (zsh - z) * mu_ref[...]

    w3 = RWKV_W
    r = z[:, 0:w3]
    k = z[:, w3:2 * w3]
    v = z[:, 2 * w3:3 * w3]
    o = 3 * w3
    wd = z[:, o:o + W_LORA]
    ad = z[:, o + W_LORA:o + W_LORA + A_LORA]
    gd = z[:, o + W_LORA + A_LORA:o + W_LORA + A_LORA + G_LORA]

    wpre = w0_ref[...] + _dot(jnp.tanh(wd).astype(BF16), w2_ref[...])
    y = -wpre
    softplus = jnp.maximum(y, 0.0) + jnp.log(1.0 + jnp.exp(-jnp.abs(y)))
    w = -softplus - 0.5
    lw_ref[0] = -jnp.exp(w)
    a = _sigmoid(a0_ref[...] + _dot(ad.astype(BF16), a2_ref[...]))
    g_ref[0] = _dot(_sigmoid(gd).astype(BF16), g2_ref[...])
    if has_vres:
        vd = z[:, RWKV_COLS:RWKV_COLS + V_LORA]
        mix = _sigmoid(v0_ref[...] + _dot(vd.astype(BF16), v2_ref[...]))
        v = v + (vf_ref[0] - v) * mix
    kk = k * ksc_ref[...]
    kk = kk / jnp.maximum(jnp.sqrt(_head_sum(kk * kk)), 1e-12)
    kh = k * (1.0 + (a - 1.0) * ka_ref[...])
    r_ref[0] = r
    k_ref[0] = kh
    v_ref[0] = v
    kk_ref[0] = kk
    b_ref[0] = kk * a
    bonus_ref[0] = _head_sum(r * kh * rk_ref[...]) * v


def _in_proj(x, mod, gn, wpc, wr, wg, mu, w0, w2, a0, a2, g2, ksc, ka, rk, vres, ts=256):
    b_, s_, d = x.shape
    zc = wr.shape[1]
    has_vres = vres is not None
    const = lambda b, i: (0, 0)
    tile = lambda b, i: (b, i, 0)
    vec = pl.BlockSpec((1, RWKV_W), const)
    in_specs = [pl.BlockSpec((1, ts, d), tile),
                pl.BlockSpec((1, 3, d), lambda b, i: (b, 0, 0)),
                pl.BlockSpec((1, d), const),
                pl.BlockSpec(wpc.shape, const),
                pl.BlockSpec(wr.shape, const),
                pl.BlockSpec(wg.shape, const),
                pl.BlockSpec((1, zc), const),
                vec, pl.BlockSpec((W_LORA, RWKV_W), const),
                vec, pl.BlockSpec((A_LORA, RWKV_W), const),
                pl.BlockSpec((G_LORA, RWKV_W), const),
                vec, vec, vec]
    args = [x, mod, gn, wpc, wr, wg, mu, w0, w2, a0, a2, g2, ksc, ka, rk]
    if has_vres:
        v_first, v0, v2 = vres
        in_specs += [pl.BlockSpec((1, ts, RWKV_W), tile), vec,
                     pl.BlockSpec((V_LORA, RWKV_W), const)]
        args += [v_first, v0, v2]
    out = jax.ShapeDtypeStruct((b_, s_, RWKV_W), F32)
    return pl.pallas_call(
        functools.partial(_inproj_kernel, has_vres),
        grid=(b_, s_ // ts),
        in_specs=in_specs,
        out_specs=[pl.BlockSpec((1, ts, wpc.shape[1]), tile),
                   pl.BlockSpec((1, ts, wg.shape[1]), tile)] + [pl.BlockSpec((1, ts, RWKV_W), tile)] * 8,
        out_shape=[jax.ShapeDtypeStruct((b_, s_, wpc.shape[1]), F32),
                   jax.ShapeDtypeStruct((b_, s_, wg.shape[1]), BF16)] + [out] * 8,
        scratch_shapes=[pltpu.VMEM((SUBLANES, zc), F32)],
        compiler_params=_params(("parallel", "arbitrary")),
        name="in_proj",
    )(*args)


def _split3(x):
    hi = x.astype(BF16)
    r1 = x - hi.astype(F32)
    mid = r1.astype(BF16)
    lo = (r1 - mid.astype(F32)).astype(BF16)
    return hi, mid, lo


def _dot_nt(a, b):
    return lax.dot_general(a, b, (((1,), (1,)), ((), ())), preferred_element_type=F32)


def _dot_tn(a, b):
    return lax.dot_general(a, b, (((0,), (0,)), ((), ())), preferred_element_type=F32)


CHUNKS_PER_STEP = 4


def _wkv_kernel(r_ref, lw_ref, k_ref, v_ref, kk_ref, b_ref, o_ref, z_ref):
    @pl.when(pl.program_id(1) == 0)
    def _():
        z_ref[...] = jnp.zeros_like(z_ref)

    c_ = CHUNK
    n_ = RWKV_HEAD
    nc = CHUNKS_PER_STEP
    n_pairs = RWKV_HEADS // 2
    n_steps = r_ref.shape[1] // (c_ * nc)
    ri = lax.broadcasted_iota(jnp.int32, (c_, c_), 0)
    ci = lax.broadcasted_iota(jnp.int32, (c_, c_), 1)
    tril_incl = (ri >= ci)
    tril_strict = (ri > ci)
    rs = lax.broadcasted_iota(jnp.int32, (nc * c_, nc * c_), 0)
    cs = lax.broadcasted_iota(jnp.int32, (nc * c_, nc * c_), 1)
    tri_b = jnp.where((rs >= cs) & ((rs // c_) == (cs // c_)), 1.0, 0.0).astype(BF16)
    rl = lax.broadcasted_iota(jnp.int32, (LANES, LANES), 0)
    cl = lax.broadcasted_iota(jnp.int32, (LANES, LANES), 1)
    eye_pair = (rl == cl)
    same_head = ((rl < n_) == (cl < n_))
    head0 = lax.broadcasted_iota(jnp.int32, (1, LANES), 1) < n_
    pl_ = lambda a, p: a[:, p * LANES:(p + 1) * LANES]
    ch_ = lambda a, q: a[q * c_:(q + 1) * c_]

    def step_body(st, carry):
        rows = pl.ds(pl.multiple_of(st * (nc * c_), nc * c_), nc * c_)
        lw = lw_ref[0, rows, :]
        hi, mid, lo = _split3(lw)
        cum = _dot(tri_b, hi) + _dot(tri_b, mid) + _dot(tri_b, lo)
        cum_last = jnp.concatenate(
            [jnp.broadcast_to(cum[(q + 1) * c_ - 1:(q + 1) * c_, :], (c_, cum.shape[1])) for q in range(nc)],
            axis=0)
        w_inv = jnp.exp(-cum)
        w_tail = jnp.exp(cum_last - cum)
        w_chunk = jnp.exp(cum_last)
        kk = kk_ref[0, rows, :]
        bb = b_ref[0, rows, :]
        kx = k_ref[0, rows, :]
        at32 = -kk * jnp.exp(cum - lw)
        rt32 = r_ref[0, rows, :] * jnp.exp(cum)
        at = at32.astype(BF16)
        rt = rt32.astype(BF16)
        bt = (bb * w_inv).astype(BF16)
        kt = (kx * w_inv).astype(BF16)
        bh = (bb * w_tail).astype(BF16)
        kh = (kx * w_tail).astype(BF16)
        v32 = v_ref[0, rows, :]
        vrot = jnp.concatenate([pltpu.roll(pl_(v32, p), shift=n_, axis=1) for p in range(n_pairs)],
                               axis=1).astype(BF16)

        chains = [(q, p, h) for q in range(nc) for p in range(n_pairs) for h in range(2)]
        aa = []
        for q, p, h in chains:
            lhs = jnp.concatenate([ch_(pl_(at, p), q), ch_(pl_(rt, p), q)], axis=0)
            rhs = jnp.concatenate([ch_(pl_(bt, p), q), ch_(pl_(kt, p), q)], axis=0)
            keep = head0 if h == 0 else jnp.logical_not(head0)
            aa.append(_dot_nt(jnp.where(keep, lhs, jnp.zeros_like(lhs)), rhs))
        a_ab = [jnp.where(tril_strict, a[:c_, :c_], 0.0).astype(BF16) for a in aa]
        a_rb = [jnp.where(tril_incl, a[c_:, :c_], 0.0).astype(BF16) for a in aa]
        a_k = [jnp.concatenate([jnp.where(tril_strict, a[:c_, c_:], 0.0),
                                jnp.where(tril_incl, a[c_:, c_:], 0.0)], axis=0).astype(BF16)
               for a in aa]
        av = [_dot(a_k[i], ch_(pl_(vrot, p), q)) for i, (q, p, h) in enumerate(chains)]
        own = [head0 if h == 0 else jnp.logical_not(head0) for q, p, h in chains]
        xs = [jnp.where(own[i], ch_(pl_(at32, p), q), av[i][:c_])
              for i, (q, p, h) in enumerate(chains)]
        ps = a_ab
        for it in range(6):
            xs = [x + _dot(pw, x.astype(BF16)) for x, pw in zip(xs, ps)]
            if it < 5:
                ps = [_dot(pw, pw).astype(BF16) for pw in ps]
        qos = [jnp.where(own[i], ch_(pl_(rt32, p), q), av[i][c_:]) + _dot(a_rb[i], xs[i].astype(BF16))
               for i, (q, p, h) in enumerate(chains)]
        ops = []
        for q in range(nc):
            for p in range(n_pairs):
                i0 = (q * n_pairs + p) * 2
                g_pair = jnp.where(head0, xs[i0], xs[i0 + 1])
                u_swap = jnp.where(head0, xs[i0 + 1], xs[i0])
                q_pair = jnp.where(head0, qos[i0], qos[i0 + 1])
                o_swap = jnp.where(head0, qos[i0 + 1], qos[i0])
                gu = jnp.concatenate([g_pair, u_swap], axis=1).astype(BF16)
                m1 = _dot_tn(ch_(pl_(bh, p), q), gu)
                m2 = _dot_tn(ch_(pl_(kh, p), q), ch_(pl_(vrot, p), q))
                wc = ch_(pl_(w_chunk, p), q)[0:1, :]
                phi_t = jnp.where(eye_pair, wc, 0.0) + jnp.where(same_head, m1[:, :LANES], 0.0)
                psi_t = jnp.where(same_head, 0.0, m1[:, LANES:] + m2)
                ops.append((q_pair.astype(BF16), o_swap, phi_t.astype(BF16), psi_t))
        for p in range(n_pairs):
            z = z_ref[p]
            outs = []
            for q in range(nc):
                q_pair, o_swap, phi_t, psi_t = ops[q * n_pairs + p]
                zb = z.astype(BF16)
                outs.append(pltpu.roll(_dot(q_pair, zb) + o_swap, shift=n_, axis=1))
                z = _dot(phi_t, zb) + psi_t
            z_ref[p] = z
            o_ref[0, rows, p * LANES:(p + 1) * LANES] = jnp.concatenate(outs, axis=0)
        return carry

    lax.fori_loop(0, n_steps, step_body, 0)


def _wkv_scan(r, lw, k, v, kk, bb, tc=512):
    b_, s_, w_ = r.shape
    tile = lambda b, i: (b, i, 0)
    spec = pl.BlockSpec((1, tc, w_), tile)
    return pl.pallas_call(
        _wkv_kernel,
        grid=(b_, s_ // tc),
        in_specs=[spec] * 6,
        out_specs=spec,
        out_shape=jax.ShapeDtypeStruct((b_, s_, w_), F32),
        scratch_shapes=[pltpu.VMEM((RWKV_HEADS // 2, LANES, LANES), F32)],
        compiler_params=_params(("parallel", "arbitrary")),
        name="wkv_scan",
    )(r, lw, k, v, kk, bb)


def _post_kernel(x_ref, mod_ref, zpc_ref, halo_ref, zg_ref, o_ref, bonus_ref, g_ref,
                 poolw_ref, pscale_ref, pproj_ref, convw_ref, convb_ref, lng_ref, lnb_ref,
                 cproj_ref, lnxg_ref, lnxb_ref, rproj_ref, wo_ref, out_ref, ext_ref):
    i = pl.program_id(1)
    ts = x_ref.shape[1]
    halo = jnp.where(i > 0, halo_ref[0], 0.0)
    cur = zpc_ref[0]

    up = cur[:, :POOL_W]
    ext = jnp.concatenate([halo[:, :POOL_W], up], axis=0)
    e1 = ext + pltpu.roll(ext, shift=1, axis=0)
    e2 = e1 + pltpu.roll(e1, shift=2, axis=0)
    e3 = e2 + pltpu.roll(e2, shift=4, axis=0)
    e4 = e3 + pltpu.roll(e3, shift=8, axis=0)
    lane = lax.broadcasted_iota(jnp.int32, (1, POOL_W), 1)
    grp = lane // POOL_GROUP_W
    pooled = jnp.where(grp == 0, e1, jnp.where(grp == 1, e2, jnp.where(grp == 2, e3, e4)))[HALO:]
    win = jnp.where(grp == 0, 2.0, jnp.where(grp == 1, 4.0, jnp.where(grp == 2, 8.0, 16.0)))
    pos = (i * ts + 1 + lax.broadcasted_iota(jnp.int32, (ts, 1), 0)).astype(F32)
    p = pooled / jnp.minimum(pos, win) - up
    p = _dot(p.astype(BF16), poolw_ref[...]) * pscale_ref[...]
    y_pool = _dot(p.astype(BF16), pproj_ref[...])

    za = jnp.concatenate([halo[:, POOL_W:POOL_W + CONV_W], cur[:, POOL_W:POOL_W + CONV_W]], axis=0)
    zb = jnp.concatenate([halo[:, POOL_W + CONV_W:], cur[:, POOL_W + CONV_W:]], axis=0)
    ext_ref[...] = za * _sigmoid(zb)
    acc = jnp.zeros((ts, CONV_W), F32) + convb_ref[...]
    for j in range(CONV_K):
        st = HALO - (CONV_K - 1) + j
        acc = acc + ext_ref[st:st + ts, :] * convw_ref[j:j + 1, :]
    mean = jnp.mean(acc, axis=-1, keepdims=True)
    dev = acc - mean
    var = jnp.mean(dev * dev, axis=-1, keepdims=True)
    un = dev * lax.rsqrt(var + LN_EPS) * lng_ref[...] + lnb_ref[...]
    y_conv = _dot(_silu(un).astype(BF16), cproj_ref[...])

    o = o_ref[0]
    inv_n = 1.0 / RWKV_HEAD
    mu = _head_sum(o) * inv_n
    od = o - mu
    ovar = _head_sum(od * od) * inv_n
    on = od * lax.rsqrt(ovar + GN_EPS) * lnxg_ref[...] + lnxb_ref[...]
    y_rwkv = _dot(((on + bonus_ref[0]) * g_ref[0]).astype(BF16), rproj_ref[...])

    d = x_ref.shape[2]
    zg = zg_ref[0]
    merged = (zg[:, :d].astype(F32) * y_pool + zg[:, d:2 * d].astype(F32) * y_conv
              + zg[:, 2 * d:].astype(F32) * y_rwkv)
    out_ref[0] = x_ref[0] + mod_ref[0, 2:3, :] * _dot(merged.astype(BF16), wo_ref[...])


def _post(x, mod, zpc, zg, o, bonus, g, poolw, pscale, pproj, convw, convb, lng, lnb, cproj,
          lnxg, lnxb, rproj, wo, ts=256):
    b_, s_, d = x.shape
    const = lambda b, i: (0, 0)
    tile = lambda b, i: (b, i, 0)
    full = lambda a: pl.BlockSpec(a.shape, const)
    in_specs = [pl.BlockSpec((1, ts, d), tile),
                pl.BlockSpec((1, 3, d), lambda b, i: (b, 0, 0)),
                pl.BlockSpec((1, ts, zpc.shape[2]), tile),
                pl.BlockSpec((1, HALO, zpc.shape[2]),
                             lambda b, i: (b, jnp.maximum(i * (ts // HALO) - 1, 0), 0)),
                pl.BlockSpec((1, ts, zg.shape[2]), tile),
                pl.BlockSpec((1, ts, RWKV_W), tile),
                pl.BlockSpec((1, ts, RWKV_W), tile),
                pl.BlockSpec((1, ts, RWKV_W), tile)]
    weights = [poolw, pscale, pproj, convw, convb, lng, lnb, cproj, lnxg, lnxb, rproj, wo]
    in_specs += [full(a) for a in weights]
    return pl.pallas_call(
        _post_kernel,
        grid=(b_, s_ // ts),
        in_specs=in_specs,
        out_specs=pl.BlockSpec((1, ts, d), tile),
        out_shape=jax.ShapeDtypeStruct((b_, s_, d), F32),
        scratch_shapes=[pltpu.VMEM((HALO + ts, CONV_W), F32)],
        compiler_params=_params(("parallel", "parallel")),
        name="mix_merge",
    )(x, mod, zpc, zpc, zg, o, bonus, g, *weights)


def _router_kernel(x_ref, mod_ref, g_ref, wr_ref, gate_ref, h_ref):
    h = _ada_rms(x_ref[0], mod_ref[0], g_ref[...])
    h_ref[0] = h.astype(BF16)
    logits = jnp.dot(h, wr_ref[...], preferred_element_type=F32, precision=lax.Precision.HIGHEST)
    n_e = logits.shape[1]
    lane = lax.broadcasted_iota(jnp.int32, logits.shape, 1)
    m1 = jnp.max(logits, axis=-1, keepdims=True)
    i1 = jnp.min(jnp.where(logits == m1, lane, n_e), axis=-1, keepdims=True)
    sel1 = lane == i1
    rest = jnp.where(sel1, -jnp.inf, logits)
    m2 = jnp.max(rest, axis=-1, keepdims=True)
    i2 = jnp.min(jnp.where(rest == m2, lane, n_e), axis=-1, keepdims=True)
    sel2 = lane == i2
    e2 = jnp.exp(m2 - m1)
    w1 = 1.0 / (1.0 + e2)
    gate_ref[0] = jnp.where(sel1, w1, 0.0) + jnp.where(sel2, e2 * w1, 0.0)


def _router(x, mod, g, wr, ts=512):
    b_, s_, d = x.shape
    n_e = wr.shape[1]
    tile = lambda b, i: (b, i, 0)
    return pl.pallas_call(
        _router_kernel,
        grid=(b_, s_ // ts),
        in_specs=[pl.BlockSpec((1, ts, d), tile),
                  pl.BlockSpec((1, 3, d), lambda b, i: (b, 0, 0)),
                  pl.BlockSpec((1, d), lambda b, i: (0, 0)),
                  pl.BlockSpec(wr.shape, lambda b, i: (0, 0))],
        out_specs=[pl.BlockSpec((1, ts, n_e), tile), pl.BlockSpec((1, ts, d), tile)],
        out_shape=[jax.ShapeDtypeStruct((b_, s_, n_e), F32), jax.ShapeDtypeStruct((b_, s_, d), BF16)],
        compiler_params=_params(("parallel", "parallel")),
        name="moe_router",
    )(x, mod, g, wr)


PIECE = 128
SUB = 256


def _moe_kernel(cnt_ref, lo_ref, hi_ref, h_ref, rank_em_ref, rank_tm_ref, gate_ref,
                w1_ref, w3_ref, w2_ref, y_ref, xs_ref, acc_ref, tmp_ref):
    b = pl.program_id(0)
    e = pl.program_id(1)
    f = pl.program_id(2)
    n_e = pl.num_programs(1)
    max_pieces = h_ref.shape[0] // PIECE
    pair = b * n_e + e
    n_pieces = (cnt_ref[pair] + (PIECE - 1)) // PIECE
    n_big = n_pieces // 2
    has_tail = (n_pieces % 2) == 1
    sub_rows = lambda s: pl.ds(pl.multiple_of(s * SUB, SUB), SUB)

    def window_bounds(j0, m):
        return lo_ref[pair * max_pieces + j0], hi_ref[pair * max_pieces + j0 + m // PIECE - 1] + 1

    def for_groups(fn):
        def big(i, carry):
            fn(2 * i, 2 * PIECE)
            return carry
        lax.fori_loop(0, n_big, big, 0)

        @pl.when(has_tail)
        def _():
            fn(2 * n_big, PIECE)

    def rows_of(j0, m):
        return pl.ds(pl.multiple_of(j0 * PIECE, PIECE), m)

    @pl.when((e == 0) & (f == 0))
    def _():
        y_ref[...] = jnp.zeros_like(y_ref)

    def gather(j0, m):
        slot = lax.broadcasted_iota(jnp.int32, (m, 1), 0) + j0 * PIECE
        tmp_ref[0:m, :] = jnp.zeros((m, tmp_ref.shape[1]), F32)

        def window(s, carry):
            onehot = jnp.where(rank_em_ref[0, 0, s] == slot, 1.0, 0.0).astype(BF16)
            tmp_ref[0:m, :] += _dot(onehot, h_ref[sub_rows(s), :])
            return carry
        lax.fori_loop(*window_bounds(j0, m), window, 0)
        xs_ref[rows_of(j0, m), :] = tmp_ref[0:m, :].astype(BF16)

    @pl.when(f == 0)
    def _():
        for_groups(gather)

    def expert(j0, m):
        rows = rows_of(j0, m)
        xj = xs_ref[rows, :]
        act = _silu(_dot(xj, w1_ref[0])) * _dot(xj, w3_ref[0])
        part = _dot(act.astype(BF16), w2_ref[0])

        @pl.when(f == 0)
        def _():
            acc_ref[rows, :] = part

        @pl.when(f > 0)
        def _():
            acc_ref[rows, :] += part
    for_groups(expert)

    def scatter(j0, m):
        lane_e = lax.broadcasted_iota(jnp.int32, (1, n_e), 1) == e
        slot = lax.broadcasted_iota(jnp.int32, (1, m), 1) + j0 * PIECE
        yj = acc_ref[rows_of(j0, m), :].astype(BF16)

        def window(s, carry):
            rows = sub_rows(s)
            rank_col = jnp.sum(jnp.where(lane_e, rank_tm_ref[rows, :], 0), axis=-1, keepdims=True)
            gate_col = jnp.sum(jnp.where(lane_e, gate_ref[rows, :], 0.0), axis=-1, keepdims=True)
            onehot_t = jnp.where(rank_col == slot, 1.0, 0.0).astype(BF16)
            y_ref[rows, :] += gate_col * _dot(onehot_t, yj)
            return carry
        lax.fori_loop(*window_bounds(j0, m), window, 0)

    @pl.when(f == pl.num_programs(2) - 1)
    def _():
        for_groups(scatter)


def _moe(h, gate, w1, w3, w2, tb=2048, tf=1792):
    t_, d = h.shape
    n_e, _, ff = w1.shape
    nb = t_ // tb
    n_sub = tb // SUB
    max_pieces = tb // PIECE
    sel = (gate != 0.0).reshape(nb, tb, n_e).astype(jnp.int32)
    rank_tm = jnp.where(sel > 0, jnp.cumsum(sel, axis=1) - 1, -1)
    ends = jnp.cumsum(jnp.sum(sel.reshape(nb, n_sub, SUB, n_e), axis=2), axis=1)
    ends = jnp.transpose(ends, (0, 2, 1))
    cnt = ends[:, :, -1]
    starts = jnp.concatenate([jnp.zeros_like(ends[:, :, :1]), ends[:, :, :-1]], axis=2)
    first = (jnp.arange(max_pieces, dtype=jnp.int32) * PIECE)[None, None, :, None]
    last = jnp.minimum(first + PIECE, cnt[:, :, None, None])
    win_lo = jnp.sum((ends[:, :, None, :] <= first).astype(jnp.int32), axis=-1)
    win_hi = jnp.sum((starts[:, :, None, :] < last).astype(jnp.int32), axis=-1) - 1
    rank_em = jnp.transpose(rank_tm, (0, 2, 1)).reshape(nb, n_e, n_sub, 1, SUB)
    once = pl.Buffered(1)
    grid_spec = pltpu.PrefetchScalarGridSpec(
        num_scalar_prefetch=3,
        grid=(nb, n_e, ff // tf),
        in_specs=[pl.BlockSpec((tb, d), lambda b, e, f, *_: (b, 0), pipeline_mode=once),
                  pl.BlockSpec((1, 1, n_sub, 1, SUB), lambda b, e, f, *_: (b, e, 0, 0, 0)),
                  pl.BlockSpec((tb, n_e), lambda b, e, f, *_: (b, 0), pipeline_mode=once),
                  pl.BlockSpec((tb, n_e), lambda b, e, f, *_: (b, 0), pipeline_mode=once),
                  pl.BlockSpec((1, d, tf), lambda b, e, f, *_: (e, 0, f)),
                  pl.BlockSpec((1, d, tf), lambda b, e, f, *_: (e, 0, f)),
                  pl.BlockSpec((1, tf, d), lambda b, e, f, *_: (e, f, 0))],
        out_specs=pl.BlockSpec((tb, d), lambda b, e, f, *_: (b, 0), pipeline_mode=once),
        scratch_shapes=[pltpu.VMEM((tb, d), BF16), pltpu.VMEM((tb, d), F32),
                        pltpu.VMEM((2 * PIECE, d), F32)])
    return pl.pallas_call(
        _moe_kernel,
        grid_spec=grid_spec,
        out_shape=jax.ShapeDtypeStruct((t_, d), F32),
        compiler_params=_params(("parallel", "arbitrary", "arbitrary")),
        name="moe_sparse",
    )(cnt.reshape(-1), win_lo.reshape(-1), win_hi.reshape(-1), h, rank_em,
      rank_tm.reshape(t_, n_e), gate, w1, w3, w2)


def _ffn_kernel(x_ref, mod_ref, g_ref, w1_ref, w3_ref, w2_ref, out_ref, h_ref, acc_ref):
    f = pl.program_id(2)

    @pl.when(f == 0)
    def _():
        h_ref[...] = _ada_rms(x_ref[0], mod_ref[0], g_ref[...]).astype(BF16)
        acc_ref[...] = jnp.zeros_like(acc_ref)

    h = h_ref[...]
    act = _silu(_dot(h, w1_ref[...])) * _dot(h, w3_ref[...])
    acc_ref[...] += _dot(act.astype(BF16), w2_ref[...])

    @pl.when(f == pl.num_programs(2) - 1)
    def _():
        out_ref[0] = x_ref[0] + mod_ref[0, 2:3, :] * acc_ref[...]


def _ffn(x, mod, g, w1, w3, w2, tm=512, tf=1408):
    b_, s_, d = x.shape
    ff = w1.shape[1]
    tile = lambda b, i, f: (b, i, 0)
    return pl.pallas_call(
        _ffn_kernel,
        grid=(b_, s_ // tm, ff // tf),
        in_specs=[pl.BlockSpec((1, tm, d), tile),
                  pl.BlockSpec((1, 3, d), lambda b, i, f: (b, 0, 0)),
                  pl.BlockSpec((1, d), lambda b, i, f: (0, 0)),
                  pl.BlockSpec((d, tf), lambda b, i, f: (0, f)),
                  pl.BlockSpec((d, tf), lambda b, i, f: (0, f)),
                  pl.BlockSpec((tf, d), lambda b, i, f: (f, 0))],
        out_specs=pl.BlockSpec((1, tm, d), tile),
        out_shape=jax.ShapeDtypeStruct((b_, s_, d), F32),
        scratch_shapes=[pltpu.VMEM((tm, d), BF16), pltpu.VMEM((tm, d), F32)],
        compiler_params=_params(("parallel", "parallel", "arbitrary")),
        name="ffn",
    )(x, mod, g, w1, w3, w2)


def _finish_kernel(has_y, has_norm, x_ref, *rest):
    x = x_ref[0]
    if has_y:
        y_ref, mod_ref = rest[:2]
        rest = rest[2:]
        x = x + mod_ref[0, 2:3, :] * y_ref[0]
    if has_norm:
        g_ref = rest[0]
        rest = rest[1:]
        x = x * lax.rsqrt(jnp.mean(x * x, axis=-1, keepdims=True) + RMS_EPS) * g_ref[...]
    rest[0][0] = x


def _finish(x, y=None, mod=None, g=None, ts=512):
    b_, s_, d = x.shape
    tile = lambda b, i: (b, i, 0)
    in_specs = [pl.BlockSpec((1, ts, d), tile)]
    args = [x]
    if y is not None:
        in_specs += [pl.BlockSpec((1, ts, d), tile), pl.BlockSpec((1, 3, d), lambda b, i: (b, 0, 0))]
        args += [y, mod]
    if g is not None:
        in_specs.append(pl.BlockSpec((1, d), lambda b, i: (0, 0)))
        args.append(g)
    return pl.pallas_call(
        functools.partial(_finish_kernel, y is not None, g is not None),
        grid=(b_, s_ // ts),
        in_specs=in_specs,
        out_specs=pl.BlockSpec((1, ts, d), tile),
        out_shape=jax.ShapeDtypeStruct((b_, s_, d), F32),
        compiler_params=_params(("parallel", "parallel")),
        name="finish",
    )(*args)


def _block_diag(w):
    g_, c_, _ = w.shape
    out = jnp.zeros((g_ * c_, g_ * c_), w.dtype)
    for gi in range(g_):
        out = out.at[gi * c_:(gi + 1) * c_, gi * c_:(gi + 1) * c_].set(w[gi])
    return out


def kernel(x, c, ada_w, ada_b, norm_g, w_in, pool_w, pool_scale, pool_proj, conv_w, conv_b, conv_ln_g, conv_ln_b, conv_proj, rwkv_mu, rwkv_w0, rwkv_w2, rwkv_a0, rwkv_a2, rwkv_g2, rwkv_kk_scale, rwkv_ka, rwkv_rk, rwkv_lnx_g, rwkv_lnx_b, rwkv_proj, vres_w_down, vres_mu, vres_v0, vres_v2, w_o, ffn_w1, ffn_w3, ffn_w2, moe_router, moe_w1, moe_w3, moe_w2, final_norm_g):
    depth = w_in.shape[0]
    d = x.shape[2]
    row = lambda a: a.reshape(1, -1)
    mods = _ada_mods(c, ada_w, ada_b)
    v_first = None
    for l in range(depth):
        mod = mods[2 * l]
        zr_pad = jnp.zeros((d, ZR_COLS - RWKV_COLS - V_LORA), F32)
        mu_pad = jnp.zeros((ZR_COLS - RWKV_COLS - V_LORA,), F32)
        if l == 0:
            w_vd = jnp.zeros((d, V_LORA), F32)
            mu_vd = jnp.zeros((V_LORA,), F32)
        else:
            w_vd = vres_w_down[l - 1]
            mu_vd = vres_mu[l - 1]
        wr = jnp.concatenate([w_in[l][:, OFF_RWKV:OFF_GATE], w_vd, zr_pad], axis=1).astype(BF16)
        mu = jnp.concatenate([rwkv_mu[l], mu_vd, mu_pad]).reshape(1, ZR_COLS)
        wpc = w_in[l][:, :OFF_RWKV].astype(BF16)
        wg = w_in[l][:, OFF_GATE:OFF_GATE + 3 * d].astype(BF16)
        vres = None
        if l > 0:
            vres = (v_first, row(vres_v0[l - 1]), vres_v2[l - 1].astype(BF16))
        zpc, zg, r, lw, kh, v, kk, bb, g, bonus = _in_proj(
            x, mod, row(norm_g[l, 0]), wpc, wr, wg,
            mu, row(rwkv_w0[l]), rwkv_w2[l].astype(BF16), row(rwkv_a0[l]),
            rwkv_a2[l].astype(BF16), rwkv_g2[l].astype(BF16), row(rwkv_kk_scale[l]),
            row(rwkv_ka[l]), row(rwkv_rk[l]), vres)
        if l == 0:
            v_first = v
        o = _wkv_scan(r, lw, kh, v, kk, bb)
        x = _post(x, mod, zpc, zg, o, bonus, g,
                  _block_diag(pool_w[l]).astype(BF16), row(pool_scale[l]), pool_proj[l].astype(BF16),
                  conv_w[l], row(conv_b[l]), row(conv_ln_g[l]), row(conv_ln_b[l]),
                  conv_proj[l].astype(BF16), row(rwkv_lnx_g[l]), row(rwkv_lnx_b[l]),
                  rwkv_proj[l].astype(BF16), w_o[l].astype(BF16))
        mod = mods[2 * l + 1]
        gn = row(norm_g[l, 1])
        i = l // 2
        last = l == depth - 1
        if l % 2 == 0:
            x = _ffn(x, mod, gn, ffn_w1[i].astype(BF16), ffn_w3[i].astype(BF16), ffn_w2[i].astype(BF16))
            if last:
                x = _finish(x, g=row(final_norm_g))
        else:
            gate, h = _router(x, mod, gn, moe_router[i])
            n_e = gate.shape[2]
            y = _moe(h.reshape(-1, d), gate.reshape(-1, n_e), moe_w1[i].astype(BF16),
                     moe_w3[i].astype(BF16), moe_w2[i].astype(BF16)).reshape(x.shape)
            x = _finish(x, y, mod, row(final_norm_g) if last else None)
    return x
```

```python
import functools

import jax
import jax.numpy as jnp
from jax import lax
from jax.experimental import pallas as pl
from jax.experimental.pallas import tpu as pltpu

F32 = jnp.float32
BF16 = jnp.bfloat16

POOL_WINDOWS = (2, 4, 8, 16)
POOL_GROUP_W = 64
POOL_W = 256
CONV_W = 256
CONV_K = 31
RWKV_HEAD = 64
RWKV_W = 512
RWKV_HEADS = 8
W_LORA = 32
A_LORA = 32
V_LORA = 16
G_LORA = 64
RWKV_COLS = 3 * RWKV_W + W_LORA + A_LORA + G_LORA
ZR_COLS = 1792
OFF_CONV = POOL_W
OFF_RWKV = OFF_CONV + 2 * CONV_W
OFF_GATE = OFF_RWKV + RWKV_COLS
N_EXPERTS = 8
RMS_EPS = 1e-6
LN_EPS = 1e-5
GN_EPS = 64e-5

LANES = 128
SUBLANES = 8
VMEM_LIMIT = 56 * 1024 * 1024
CHUNK = 64
HALO = 32


def _sigmoid(x):
    return 1.0 / (1.0 + jnp.exp(-x))


def _silu(x):
    return x * _sigmoid(x)


def _dot(a, b):
    return jnp.dot(a, b, preferred_element_type=F32)


def _ada_rms(x, mod, g):
    y = x * lax.rsqrt(jnp.mean(x * x, axis=-1, keepdims=True) + RMS_EPS) * g
    return y * (1.0 + mod[1:2, :]) + mod[0:1, :]


def _params(sem):
    return pltpu.CompilerParams(dimension_semantics=sem, vmem_limit_bytes=VMEM_LIMIT)


def _mod_kernel(c_ref, w_ref, b_ref, o_ref):
    c = c_ref[...]
    o_ref[0] = jnp.dot(_silu(c), w_ref[0], preferred_element_type=F32,
                       precision=lax.Precision.HIGHEST) + b_ref[0]


def _ada_mods(c, ada_w, ada_b):
    b_, d = c.shape
    n_sub = ada_w.shape[0] * ada_w.shape[1]
    w = ada_w.reshape(n_sub, d, 3 * d)
    bias = ada_b.reshape(n_sub, 1, 3 * d)
    c8 = jnp.zeros((SUBLANES, d), F32).at[:b_].set(c)
    tn = 1024
    out = pl.pallas_call(
        _mod_kernel,
        grid=(n_sub, 3 * d // tn),
        in_specs=[pl.BlockSpec((SUBLANES, d), lambda s, j: (0, 0)),
                  pl.BlockSpec((1, d, tn), lambda s, j: (s, 0, j)),
                  pl.BlockSpec((1, 1, tn), lambda s, j: (s, 0, j))],
        out_specs=pl.BlockSpec((1, SUBLANES, tn), lambda s, j: (s, 0, j)),
        out_shape=jax.ShapeDtypeStruct((n_sub, SUBLANES, 3 * d), F32),
        compiler_params=_params(("parallel", "parallel")),
        name="ada_mod",
    )(c8, w, bias)
    return out[:, :b_].reshape(n_sub, b_, 3, d)


def _head_sum(x):
    low = lax.broadcasted_iota(jnp.int32, (1, LANES), 1) < RWKV_HEAD
    outs = []
    for cb in range(x.shape[1] // LANES):
        xc = x[:, cb * LANES:(cb + 1) * LANES]
        s_lo = jnp.sum(jnp.where(low, xc, 0.0), axis=-1, keepdims=True)
        s_hi = jnp.sum(jnp.where(low, 0.0, xc), axis=-1, keepdims=True)
        outs.append(jnp.where(low, s_lo, s_hi))
    return jnp.concatenate(outs, axis=-1)


def _inproj_kernel(has_vres, x_ref, mod_ref, gn_ref, wpc_ref, wr_ref, wg_ref, mu_ref, w0_ref, w2_ref,
                   a0_ref, a2_ref, g2_ref, ksc_ref, ka_ref, rk_ref, *rest):
    if has_vres:
        vf_ref, v0_ref, v2_ref = rest[:3]
        rest = rest[3:]
    zpc_ref, zg_ref, r_ref, lw_ref, k_ref, v_ref, kk_ref, b_ref, g_ref, bonus_ref, last_ref = rest

    i = pl.program_id(1)
    hb = _ada_rms(x_ref[0], mod_ref[0], gn_ref[...]).astype(BF16)
    z = _dot(hb, wr_ref[...])
    ts = z.shape[0]
    prev = jnp.where(i > 0, last_ref[SUBLANES - 1:SUBLANES, :], 0.0)
    last_ref[...] = z[ts - SUBLANES:, :]
    row = lax.broadcasted_iota(jnp.int32, (ts, 1), 0)
    zsh = jnp.where(row == 0, prev, pltpu.roll(z, shift=1, axis=0))
    z = z + (zsh - z) * mu_ref[...]

    w3 = RWKV_W
    r = z[:, 0:w3]
    k = z[:, w3:2 * w3]
    v = z[:, 2 * w3:3 * w3]
    o = 3 * w3
    wd = z[:, o:o + W_LORA]
    ad = z[:, o + W_LORA:o + W_LORA + A_LORA]
    gd = z[:, o + W_LORA + A_LORA:o + W_LORA + A_LORA + G_LORA]

    wpre = w0_ref[...] + _dot(jnp.tanh(wd).astype(BF16), w2_ref[...])
    y = -wpre
    softplus = jnp.maximum(y, 0.0) + jnp.log(1.0 + jnp.exp(-jnp.abs(y)))
    w = -softplus - 0.5
    lw_ref[0] = -jnp.exp(w)
    a = _sigmoid(a0_ref[...] + _dot(ad.astype(BF16), a2_ref[...]))
    g_ref[0] = _dot(_sigmoid(gd).astype(BF16), g2_ref[...])
    if has_vres:
        vd = z[:, RWKV_COLS:RWKV_COLS + V_LORA]
        mix = _sigmoid(v0_ref[...] + _dot(vd.astype(BF16), v2_ref[...]))
        v = v + (vf_ref[0] - v) * mix
    kk = k * ksc_ref[...]
    kk = kk / jnp.maximum(jnp.sqrt(_head_sum(kk * kk)), 1e-12)
    kh = k * (1.0 + (a - 1.0) * ka_ref[...])
    r_ref[0] = r
    k_ref[0] = kh
    v_ref[0] = v
    kk_ref[0] = kk
    b_ref[0] = kk * a
    bonus_ref[0] = _head_sum(r * kh * rk_ref[...]) * v
    zpc_ref[0] = _dot(hb, wpc_ref[...])
    zg_ref[0] = _sigmoid(_dot(hb, wg_ref[...])).astype(BF16)


def _in_proj(x, mod, gn, wpc, wr, wg, mu, w0, w2, a0, a2, g2, ksc, ka, rk, vres, ts=256):
    b_, s_, d = x.shape
    zc = wr.shape[1]
    has_vres = vres is not None
    const = lambda b, i: (0, 0)
    tile = lambda b, i: (b, i, 0)
    vec = pl.BlockSpec((1, RWKV_W), const)
    in_specs = [pl.BlockSpec((1, ts, d), tile),
                pl.BlockSpec((1, 3, d), lambda b, i: (b, 0, 0)),
                pl.BlockSpec((1, d), const),
                pl.BlockSpec(wpc.shape, const),
                pl.BlockSpec(wr.shape, const),
                pl.BlockSpec(wg.shape, const),
                pl.BlockSpec((1, zc), const),
                vec, pl.BlockSpec((W_LORA, RWKV_W), const),
                vec, pl.BlockSpec((A_LORA, RWKV_W), const),
                pl.BlockSpec((G_LORA, RWKV_W), const),
                vec, vec, vec]
    args = [x, mod, gn, wpc, wr, wg, mu, w0, w2, a0, a2, g2, ksc, ka, rk]
    if has_vres:
        v_first, v0, v2 = vres
        in_specs += [pl.BlockSpec((1, ts, RWKV_W), tile), vec,
                     pl.BlockSpec((V_LORA, RWKV_W), const)]
        args += [v_first, v0, v2]
    out = jax.ShapeDtypeStruct((b_, s_, RWKV_W), F32)
    return pl.pallas_call(
        functools.partial(_inproj_kernel, has_vres),
        grid=(b_, s_ // ts),
        in_specs=in_specs,
        out_specs=[pl.BlockSpec((1, ts, wpc.shape[1]), tile),
                   pl.BlockSpec((1, ts, wg.shape[1]), tile)] + [pl.BlockSpec((1, ts, RWKV_W), tile)] * 8,
        out_shape=[jax.ShapeDtypeStruct((b_, s_, wpc.shape[1]), F32),
                   jax.ShapeDtypeStruct((b_, s_, wg.shape[1]), BF16)] + [out] * 8,
        scratch_shapes=[pltpu.VMEM((SUBLANES, zc), F32)],
        compiler_params=_params(("parallel", "arbitrary")),
        name="in_proj",
    )(*args)


def _split3(x):
    hi = x.astype(BF16)
    r1 = x - hi.astype(F32)
    mid = r1.astype(BF16)
    lo = (r1 - mid.astype(F32)).astype(BF16)
    return hi, mid, lo


def _dot_nt(a, b):
    return lax.dot_general(a, b, (((1,), (1,)), ((), ())), preferred_element_type=F32)


def _dot_tn(a, b):
    return lax.dot_general(a, b, (((0,), (0,)), ((), ())), preferred_element_type=F32)


CHUNKS_PER_STEP = 4


def _wkv_kernel(r_ref, lw_ref, k_ref, v_ref, kk_ref, b_ref, o_ref, z_ref):
    @pl.when(pl.program_id(1) == 0)
    def _():
        z_ref[...] = jnp.zeros_like(z_ref)

    c_ = CHUNK
    n_ = RWKV_HEAD
    nc = CHUNKS_PER_STEP
    n_pairs = RWKV_HEADS // 2
    n_steps = r_ref.shape[1] // (c_ * nc)
    ri = lax.broadcasted_iota(jnp.int32, (c_, c_), 0)
    ci = lax.broadcasted_iota(jnp.int32, (c_, c_), 1)
    tril_incl = (ri >= ci)
    tril_strict = (ri > ci)
    rs = lax.broadcasted_iota(jnp.int32, (nc * c_, nc * c_), 0)
    cs = lax.broadcasted_iota(jnp.int32, (nc * c_, nc * c_), 1)
    tri_b = jnp.where((rs >= cs) & ((rs // c_) == (cs // c_)), 1.0, 0.0).astype(BF16)
    rl = lax.broadcasted_iota(jnp.int32, (LANES, LANES), 0)
    cl = lax.broadcasted_iota(jnp.int32, (LANES, LANES), 1)
    eye_pair = (rl == cl)
    same_head = ((rl < n_) == (cl < n_))
    head0 = lax.broadcasted_iota(jnp.int32, (1, LANES), 1) < n_
    pl_ = lambda a, p: a[:, p * LANES:(p + 1) * LANES]
    ch_ = lambda a, q: a[q * c_:(q + 1) * c_]

    def step_body(st, carry):
        rows = pl.ds(pl.multiple_of(st * (nc * c_), nc * c_), nc * c_)
        lw = lw_ref[0, rows, :]
        hi, mid, lo = _split3(lw)
        cum = _dot(tri_b, hi) + _dot(tri_b, mid) + _dot(tri_b, lo)
        cum_last = jnp.concatenate(
            [jnp.broadcast_to(cum[(q + 1) * c_ - 1:(q + 1) * c_, :], (c_, cum.shape[1])) for q in range(nc)],
            axis=0)
        w_inv = jnp.exp(-cum)
        w_tail = jnp.exp(cum_last - cum)
        w_chunk = jnp.exp(cum_last)
        kk = kk_ref[0, rows, :]
        bb = b_ref[0, rows, :]
        kx = k_ref[0, rows, :]
        at32 = -kk * jnp.exp(cum - lw)
        rt32 = r_ref[0, rows, :] * jnp.exp(cum)
        at = at32.astype(BF16)
        rt = rt32.astype(BF16)
        bt = (bb * w_inv).astype(BF16)
        kt = (kx * w_inv).astype(BF16)
        bh = (bb * w_tail).astype(BF16)
        kh = (kx * w_tail).astype(BF16)
        v32 = v_ref[0, rows, :]
        vrot = jnp.concatenate([pltpu.roll(pl_(v32, p), shift=n_, axis=1) for p in range(n_pairs)],
                               axis=1).astype(BF16)

        chains = [(q, p, h) for q in range(nc) for p in range(n_pairs) for h in range(2)]
        aa = []
        for q, p, h in chains:
            lhs = jnp.concatenate([ch_(pl_(at, p), q), ch_(pl_(rt, p), q)], axis=0)
            rhs = jnp.concatenate([ch_(pl_(bt, p), q), ch_(pl_(kt, p), q)], axis=0)
            keep = head0 if h == 0 else jnp.logical_not(head0)
            aa.append(_dot_nt(jnp.where(keep, lhs, jnp.zeros_like(lhs)), rhs))
        a_ab = [jnp.where(tril_strict, a[:c_, :c_], 0.0).astype(BF16) for a in aa]
        a_rb = [jnp.where(tril_incl, a[c_:, :c_], 0.0).astype(BF16) for a in aa]
        a_k = [jnp.concatenate([jnp.where(tril_strict, a[:c_, c_:], 0.0),
                                jnp.where(tril_incl, a[c_:, c_:], 0.0)], axis=0).astype(BF16)
               for a in aa]
        av = [_dot(a_k[i], ch_(pl_(vrot, p), q)) for i, (q, p, h) in enumerate(chains)]
        own = [head0 if h == 0 else jnp.logical_not(head0) for q, p, h in chains]
        xs = [jnp.where(own[i], ch_(pl_(at32, p), q), av[i][:c_])
              for i, (q, p, h) in enumerate(chains)]
        ps = a_ab
        for it in range(6):
            xs = [x + _dot(pw, x.astype(BF16)) for x, pw in zip(xs, ps)]
            if it < 5:
                ps = [_dot(pw, pw).astype(BF16) for pw in ps]
        qos = [jnp.where(own[i], ch_(pl_(rt32, p), q), av[i][c_:]) + _dot(a_rb[i], xs[i].astype(BF16))
               for i, (q, p, h) in enumerate(chains)]
        ops = []
        for q in range(nc):
            for p in range(n_pairs):
                i0 = (q * n_pairs + p) * 2
                g_pair = jnp.where(head0, xs[i0], xs[i0 + 1])
                u_swap = jnp.where(head0, xs[i0 + 1], xs[i0])
                q_pair = jnp.where(head0, qos[i0], qos[i0 + 1])
                o_swap = jnp.where(head0, qos[i0 + 1], qos[i0])
                gu = jnp.concatenate([g_pair, u_swap], axis=1).astype(BF16)
                m1 = _dot_tn(ch_(pl_(bh, p), q), gu)
                m2 = _dot_tn(ch_(pl_(kh, p), q), ch_(pl_(vrot, p), q))
                wc = ch_(pl_(w_chunk, p), q)[0:1, :]
                phi_t = jnp.where(eye_pair, wc, 0.0) + jnp.where(same_head, m1[:, :LANES], 0.0)
                psi_t = jnp.where(same_head, 0.0, m1[:, LANES:] + m2)
                ops.append((q_pair.astype(BF16), o_swap, phi_t.astype(BF16), psi_t))
        for p in range(n_pairs):
            z = z_ref[p]
            outs = []
            for q in range(nc):
                q_pair, o_swap, phi_t, psi_t = ops[q * n_pairs + p]
                zb = z.astype(BF16)
                outs.append(pltpu.roll(_dot(q_pair, zb) + o_swap, shift=n_, axis=1))
                z = _dot(phi_t, zb) + psi_t
            z_ref[p] = z
            o_ref[0, rows, p * LANES:(p + 1) * LANES] = jnp.concatenate(outs, axis=0)
        return carry

    lax.fori_loop(0, n_steps, step_body, 0)


def _wkv_scan(r, lw, k, v, kk, bb, tc=512):
    b_, s_, w_ = r.shape
    tile = lambda b, i: (b, i, 0)
    spec = pl.BlockSpec((1, tc, w_), tile)
    return pl.pallas_call(
        _wkv_kernel,
        grid=(b_, s_ // tc),
        in_specs=[spec] * 6,
        out_specs=spec,
        out_shape=jax.ShapeDtypeStruct((b_, s_, w_), F32),
        scratch_shapes=[pltpu.VMEM((RWKV_HEADS // 2, LANES, LANES), F32)],
        compiler_params=_params(("parallel", "arbitrary")),
        name="wkv_scan",
    )(r, lw, k, v, kk, bb)


def _post_kernel(x_ref, mod_ref, zpc_ref, halo_ref, zg_ref, o_ref, bonus_ref, g_ref,
                 poolw_ref, pscale_ref, pproj_ref, convw_ref, convb_ref, lng_ref, lnb_ref,
                 cproj_ref, lnxg_ref, lnxb_ref, rproj_ref, wo_ref, out_ref, ext_ref):
    i = pl.program_id(1)
    ts = x_ref.shape[1]
    halo = jnp.where(i > 0, halo_ref[0], 0.0)
    cur = zpc_ref[0]

    up = cur[:, :POOL_W]
    ext = jnp.concatenate([halo[:, :POOL_W], up], axis=0)
    e1 = ext + pltpu.roll(ext, shift=1, axis=0)
    e2 = e1 + pltpu.roll(e1, shift=2, axis=0)
    e3 = e2 + pltpu.roll(e2, shift=4, axis=0)
    e4 = e3 + pltpu.roll(e3, shift=8, axis=0)
    lane = lax.broadcasted_iota(jnp.int32, (1, POOL_W), 1)
    grp = lane // POOL_GROUP_W
    pooled = jnp.where(grp == 0, e1, jnp.where(grp == 1, e2, jnp.where(grp == 2, e3, e4)))[HALO:]
    win = jnp.where(grp == 0, 2.0, jnp.where(grp == 1, 4.0, jnp.where(grp == 2, 8.0, 16.0)))
    pos = (i * ts + 1 + lax.broadcasted_iota(jnp.int32, (ts, 1), 0)).astype(F32)
    p = pooled / jnp.minimum(pos, win) - up
    p = _dot(p.astype(BF16), poolw_ref[...]) * pscale_ref[...]
    y_pool = _dot(p.astype(BF16), pproj_ref[...])

    za = jnp.concatenate([halo[:, POOL_W:POOL_W + CONV_W], cur[:, POOL_W:POOL_W + CONV_W]], axis=0)
    zb = jnp.concatenate([halo[:, POOL_W + CONV_W:], cur[:, POOL_W + CONV_W:]], axis=0)
    u = za * _sigmoid(zb)
    ext_ref[0] = u
    for r in range(1, SUBLANES):
        ext_ref[r] = pltpu.roll(u, shift=u.shape[0] - r, axis=0)
    acc = jnp.zeros((ts, CONV_W), F32) + convb_ref[...]
    for j in range(CONV_K):
        st = HALO - (CONV_K - 1) + j
        al = st - st % SUBLANES
        acc = acc + ext_ref[st % SUBLANES, al:al + ts, :] * convw_ref[j:j + 1, :]
    mean = jnp.mean(acc, axis=-1, keepdims=True)
    dev = acc - mean
    var = jnp.mean(dev * dev, axis=-1, keepdims=True)
    un = dev * lax.rsqrt(var + LN_EPS) * lng_ref[...] + lnb_ref[...]
    y_conv = _dot(_silu(un).astype(BF16), cproj_ref[...])

    o = o_ref[0]
    inv_n = 1.0 / RWKV_HEAD
    mu = _head_sum(o) * inv_n
    od = o - mu
    ovar = _head_sum(od * od) * inv_n
    on = od * lax.rsqrt(ovar + GN_EPS) * lnxg_ref[...] + lnxb_ref[...]
    y_rwkv = _dot(((on + bonus_ref[0]) * g_ref[0]).astype(BF16), rproj_ref[...])

    d = x_ref.shape[2]
    zg = zg_ref[0]
    merged = (zg[:, :d].astype(F32) * y_pool + zg[:, d:2 * d].astype(F32) * y_conv
              + zg[:, 2 * d:].astype(F32) * y_rwkv)
    out_ref[0] = x_ref[0] + mod_ref[0, 2:3, :] * _dot(merged.astype(BF16), wo_ref[...])


def _post(x, mod, zpc, zg, o, bonus, g, poolw, pscale, pproj, convw, convb, lng, lnb, cproj,
          lnxg, lnxb, rproj, wo, ts=256):
    b_, s_, d = x.shape
    const = lambda b, i: (0, 0)
    tile = lambda b, i: (b, i, 0)
    full = lambda a: pl.BlockSpec(a.shape, const)
    in_specs = [pl.BlockSpec((1, ts, d), tile),
                pl.BlockSpec((1, 3, d), lambda b, i: (b, 0, 0)),
                pl.BlockSpec((1, ts, zpc.shape[2]), tile),
                pl.BlockSpec((1, HALO, zpc.shape[2]),
                             lambda b, i: (b, jnp.maximum(i * (ts // HALO) - 1, 0), 0)),
                pl.BlockSpec((1, ts, zg.shape[2]), tile),
                pl.BlockSpec((1, ts, RWKV_W), tile),
                pl.BlockSpec((1, ts, RWKV_W), tile),
                pl.BlockSpec((1, ts, RWKV_W), tile)]
    weights = [poolw, pscale, pproj, convw, convb, lng, lnb, cproj, lnxg, lnxb, rproj, wo]
    in_specs += [full(a) for a in weights]
    return pl.pallas_call(
        _post_kernel,
        grid=(b_, s_ // ts),
        in_specs=in_specs,
        out_specs=pl.BlockSpec((1, ts, d), tile),
        out_shape=jax.ShapeDtypeStruct((b_, s_, d), F32),
        scratch_shapes=[pltpu.VMEM((SUBLANES, HALO + ts, CONV_W), F32)],
        compiler_params=_params(("parallel", "parallel")),
        name="mix_merge",
    )(x, mod, zpc, zpc, zg, o, bonus, g, *weights)


def _router_kernel(x_ref, mod_ref, g_ref, wr_ref, gate_ref, h_ref):
    h = _ada_rms(x_ref[0], mod_ref[0], g_ref[...])
    h_ref[0] = h.astype(BF16)
    logits = jnp.dot(h, wr_ref[...], preferred_element_type=F32, precision=lax.Precision.HIGHEST)
    n_e = logits.shape[1]
    lane = lax.broadcasted_iota(jnp.int32, logits.shape, 1)
    m1 = jnp.max(logits, axis=-1, keepdims=True)
    i1 = jnp.min(jnp.where(logits == m1, lane, n_e), axis=-1, keepdims=True)
    sel1 = lane == i1
    rest = jnp.where(sel1, -jnp.inf, logits)
    m2 = jnp.max(rest, axis=-1, keepdims=True)
    i2 = jnp.min(jnp.where(rest == m2, lane, n_e), axis=-1, keepdims=True)
    sel2 = lane == i2
    e2 = jnp.exp(m2 - m1)
    w1 = 1.0 / (1.0 + e2)
    gate_ref[0] = jnp.where(sel1, w1, 0.0) + jnp.where(sel2, e2 * w1, 0.0)


def _router(x, mod, g, wr, ts=512):
    b_, s_, d = x.shape
    n_e = wr.shape[1]
    tile = lambda b, i: (b, i, 0)
    return pl.pallas_call(
        _router_kernel,
        grid=(b_, s_ // ts),
        in_specs=[pl.BlockSpec((1, ts, d), tile),
                  pl.BlockSpec((1, 3, d), lambda b, i: (b, 0, 0)),
                  pl.BlockSpec((1, d), lambda b, i: (0, 0)),
                  pl.BlockSpec(wr.shape, lambda b, i: (0, 0))],
        out_specs=[pl.BlockSpec((1, ts, n_e), tile), pl.BlockSpec((1, ts, d), tile)],
        out_shape=[jax.ShapeDtypeStruct((b_, s_, n_e), F32), jax.ShapeDtypeStruct((b_, s_, d), BF16)],
        compiler_params=_params(("parallel", "parallel")),
        name="moe_router",
    )(x, mod, g, wr)


PIECE = 128
SUB = 256


def _moe_kernel(cnt_ref, lo_ref, hi_ref, h_ref, rank_em_ref, rank_tm_ref, gate_ref,
                w1_ref, w3_ref, w2_ref, y_ref, xs_ref, acc_ref, tmp_ref):
    b = pl.program_id(0)
    e = pl.program_id(1)
    f = pl.program_id(2)
    n_e = pl.num_programs(1)
    max_pieces = h_ref.shape[0] // PIECE
    pair = b * n_e + e
    n_pieces = (cnt_ref[pair] + (PIECE - 1)) // PIECE
    n_big = n_pieces // 2
    has_tail = (n_pieces % 2) == 1
    sub_rows = lambda s: pl.ds(pl.multiple_of(s * SUB, SUB), SUB)

    def window_bounds(j0, m):
        return lo_ref[pair * max_pieces + j0], hi_ref[pair * max_pieces + j0 + m // PIECE - 1] + 1

    def for_groups(fn):
        def big(i, carry):
            fn(2 * i, 2 * PIECE)
            return carry
        lax.fori_loop(0, n_big, big, 0)

        @pl.when(has_tail)
        def _():
            fn(2 * n_big, PIECE)

    def rows_of(j0, m):
        return pl.ds(pl.multiple_of(j0 * PIECE, PIECE), m)

    @pl.when((e == 0) & (f == 0))
    def _():
        y_ref[...] = jnp.zeros_like(y_ref)

    def gather(j0, m):
        slot = lax.broadcasted_iota(jnp.int32, (m, 1), 0) + j0 * PIECE
        tmp_ref[0:m, :] = jnp.zeros((m, tmp_ref.shape[1]), F32)

        lo, hi = window_bounds(j0, m)

        def window_pair(i, carry):
            s0 = lo + 2 * i
            s1 = jnp.minimum(s0 + 1, hi - 1)
            hit1 = (rank_em_ref[0, 0, s1] == slot) & (s0 + 1 < hi)
            onehot0 = jnp.where(rank_em_ref[0, 0, s0] == slot, 1.0, 0.0).astype(BF16)
            onehot1 = jnp.where(hit1, 1.0, 0.0).astype(BF16)
            tmp_ref[0:m, :] += (_dot(onehot0, h_ref[sub_rows(s0), :])
                                + _dot(onehot1, h_ref[sub_rows(s1), :]))
            return carry
        lax.fori_loop(0, (hi - lo + 1) // 2, window_pair, 0)
        xs_ref[rows_of(j0, m), :] = tmp_ref[0:m, :].astype(BF16)

    @pl.when(f == 0)
    def _():
        for_groups(gather)

    def expert(j0, m):
        rows = rows_of(j0, m)
        xj = xs_ref[rows, :]
        act = _silu(_dot(xj, w1_ref[0])) * _dot(xj, w3_ref[0])
        part = _dot(act.astype(BF16), w2_ref[0])

        @pl.when(f == 0)
        def _():
            acc_ref[rows, :] = part

        @pl.when(f > 0)
        def _():
            acc_ref[rows, :] += part
    for_groups(expert)

    def scatter(j0, m):
        lane_e = lax.broadcasted_iota(jnp.int32, (1, n_e), 1) == e
        slot = lax.broadcasted_iota(jnp.int32, (1, m), 1) + j0 * PIECE
        yj = acc_ref[rows_of(j0, m), :].astype(BF16)

        lo, hi = window_bounds(j0, m)

        def contribution(s, live):
            rows = sub_rows(s)
            rank_col = jnp.sum(jnp.where(lane_e, rank_tm_ref[rows, :], 0), axis=-1, keepdims=True)
            gate_col = jnp.sum(jnp.where(lane_e, gate_ref[rows, :], 0.0), axis=-1, keepdims=True)
            onehot_t = jnp.where((rank_col == slot) & live, 1.0, 0.0).astype(BF16)
            return gate_col * _dot(onehot_t, yj)

        def window_pair(i, carry):
            s0 = lo + 2 * i
            s1 = jnp.minimum(s0 + 1, hi - 1)
            c0 = contribution(s0, True)
            c1 = contribution(s1, s0 + 1 < hi)
            y_ref[sub_rows(s0), :] += c0
            y_ref[sub_rows(s1), :] += c1
            return carry
        lax.fori_loop(0, (hi - lo + 1) // 2, window_pair, 0)

    @pl.when(f == pl.num_programs(2) - 1)
    def _():
        for_groups(scatter)


def _moe(h, gate, w1, w3, w2, tb=2048, tf=1792):
    t_, d = h.shape
    n_e, _, ff = w1.shape
    nb = t_ // tb
    n_sub = tb // SUB
    max_pieces = tb // PIECE
    sel = (gate != 0.0).reshape(nb, tb, n_e).astype(jnp.int32)
    rank_tm = jnp.where(sel > 0, jnp.cumsum(sel, axis=1) - 1, -1)
    ends = jnp.cumsum(jnp.sum(sel.reshape(nb, n_sub, SUB, n_e), axis=2), axis=1)
    ends = jnp.transpose(ends, (0, 2, 1))
    cnt = ends[:, :, -1]
    starts = jnp.concatenate([jnp.zeros_like(ends[:, :, :1]), ends[:, :, :-1]], axis=2)
    first = (jnp.arange(max_pieces, dtype=jnp.int32) * PIECE)[None, None, :, None]
    last = jnp.minimum(first + PIECE, cnt[:, :, None, None])
    win_lo = jnp.sum((ends[:, :, None, :] <= first).astype(jnp.int32), axis=-1)
    win_hi = jnp.sum((starts[:, :, None, :] < last).astype(jnp.int32), axis=-1) - 1
    rank_em = jnp.transpose(rank_tm, (0, 2, 1)).reshape(nb, n_e, n_sub, 1, SUB)
    once = pl.Buffered(1)
    grid_spec = pltpu.PrefetchScalarGridSpec(
        num_scalar_prefetch=3,
        grid=(nb, n_e, ff // tf),
        in_specs=[pl.BlockSpec((tb, d), lambda b, e, f, *_: (b, 0), pipeline_mode=once),
                  pl.BlockSpec((1, 1, n_sub, 1, SUB), lambda b, e, f, *_: (b, e, 0, 0, 0)),
                  pl.BlockSpec((tb, n_e), lambda b, e, f, *_: (b, 0), pipeline_mode=once),
                  pl.BlockSpec((tb, n_e), lambda b, e, f, *_: (b, 0), pipeline_mode=once),
                  pl.BlockSpec((1, d, tf), lambda b, e, f, *_: (e, 0, f)),
                  pl.BlockSpec((1, d, tf), lambda b, e, f, *_: (e, 0, f)),
                  pl.BlockSpec((1, tf, d), lambda b, e, f, *_: (e, f, 0))],
        out_specs=pl.BlockSpec((tb, d), lambda b, e, f, *_: (b, 0), pipeline_mode=once),
        scratch_shapes=[pltpu.VMEM((tb, d), BF16), pltpu.VMEM((tb, d), F32),
                        pltpu.VMEM((2 * PIECE, d), F32)])
    return pl.pallas_call(
        _moe_kernel,
        grid_spec=grid_spec,
        out_shape=jax.ShapeDtypeStruct((t_, d), F32),
        compiler_params=_params(("parallel", "arbitrary", "arbitrary")),
        name="moe_sparse",
    )(cnt.reshape(-1), win_lo.reshape(-1), win_hi.reshape(-1), h, rank_em,
      rank_tm.reshape(t_, n_e), gate, w1, w3, w2)


def _ffn_kernel(x_ref, mod_ref, g_ref, w1_ref, w3_ref, w2_ref, out_ref, h_ref, acc_ref):
    f = pl.program_id(2)

    @pl.when(f == 0)
    def _():
        h_ref[...] = _ada_rms(x_ref[0], mod_ref[0], g_ref[...]).astype(BF16)
        acc_ref[...] = jnp.zeros_like(acc_ref)

    h = h_ref[...]
    act = _silu(_dot(h, w1_ref[...])) * _dot(h, w3_ref[...])
    acc_ref[...] += _dot(act.astype(BF16), w2_ref[...])

    @pl.when(f == pl.num_programs(2) - 1)
    def _():
        out_ref[0] = x_ref[0] + mod_ref[0, 2:3, :] * acc_ref[...]


def _ffn(x, mod, g, w1, w3, w2, tm=512, tf=1408):
    b_, s_, d = x.shape
    ff = w1.shape[1]
    tile = lambda b, i, f: (b, i, 0)
    return pl.pallas_call(
        _ffn_kernel,
        grid=(b_, s_ // tm, ff // tf),
        in_specs=[pl.BlockSpec((1, tm, d), tile),
                  pl.BlockSpec((1, 3, d), lambda b, i, f: (b, 0, 0)),
                  pl.BlockSpec((1, d), lambda b, i, f: (0, 0)),
                  pl.BlockSpec((d, tf), lambda b, i, f: (0, f)),
                  pl.BlockSpec((d, tf), lambda b, i, f: (0, f)),
                  pl.BlockSpec((tf, d), lambda b, i, f: (f, 0))],
        out_specs=pl.BlockSpec((1, tm, d), tile),
        out_shape=jax.ShapeDtypeStruct((b_, s_, d), F32),
        scratch_shapes=[pltpu.VMEM((tm, d), BF16), pltpu.VMEM((tm, d), F32)],
        compiler_params=_params(("parallel", "parallel", "arbitrary")),
        name="ffn",
    )(x, mod, g, w1, w3, w2)


def _finish_kernel(has_y, has_norm, x_ref, *rest):
    x = x_ref[0]
    if has_y:
        y_ref, mod_ref = rest[:2]
        rest = rest[2:]
        x = x + mod_ref[0, 2:3, :] * y_ref[0]
    if has_norm:
        g_ref = rest[0]
        rest = rest[1:]
        x = x * lax.rsqrt(jnp.mean(x * x, axis=-1, keepdims=True) + RMS_EPS) * g_ref[...]
    rest[0][0] = x


def _finish(x, y=None, mod=None, g=None, ts=512):
    b_, s_, d = x.shape
    tile = lambda b, i: (b, i, 0)
    in_specs = [pl.BlockSpec((1, ts, d), tile)]
    args = [x]
    if y is not None:
        in_specs += [pl.BlockSpec((1, ts, d), tile), pl.BlockSpec((1, 3, d), lambda b, i: (b, 0, 0))]
        args += [y, mod]
    if g is not None:
        in_specs.append(pl.BlockSpec((1, d), lambda b, i: (0, 0)))
        args.append(g)
    return pl.pallas_call(
        functools.partial(_finish_kernel, y is not None, g is not None),
        grid=(b_, s_ // ts),
        in_specs=in_specs,
        out_specs=pl.BlockSpec((1, ts, d), tile),
        out_shape=jax.ShapeDtypeStruct((b_, s_, d), F32),
        compiler_params=_params(("parallel", "parallel")),
        name="finish",
    )(*args)


def _block_diag(w):
    g_, c_, _ = w.shape
    out = jnp.zeros((g_ * c_, g_ * c_), w.dtype)
    for gi in range(g_):
        out = out.at[gi * c_:(gi + 1) * c_, gi * c_:(gi + 1) * c_].set(w[gi])
    return out


def kernel(x, c, ada_w, ada_b, norm_g, w_in, pool_w, pool_scale, pool_proj, conv_w, conv_b, conv_ln_g, conv_ln_b, conv_proj, rwkv_mu, rwkv_w0, rwkv_w2, rwkv_a0, rwkv_a2, rwkv_g2, rwkv_kk_scale, rwkv_ka, rwkv_rk, rwkv_lnx_g, rwkv_lnx_b, rwkv_proj, vres_w_down, vres_mu, vres_v0, vres_v2, w_o, ffn_w1, ffn_w3, ffn_w2, moe_router, moe_w1, moe_w3, moe_w2, final_norm_g):
    depth = w_in.shape[0]
    d = x.shape[2]
    row = lambda a: a.reshape(1, -1)
    mods = _ada_mods(c, ada_w, ada_b)
    v_first = None
    for l in range(depth):
        mod = mods[2 * l]
        zr_pad = jnp.zeros((d, ZR_COLS - RWKV_COLS - V_LORA), F32)
        mu_pad = jnp.zeros((ZR_COLS - RWKV_COLS - V_LORA,), F32)
        if l == 0:
            w_vd = jnp.zeros((d, V_LORA), F32)
            mu_vd = jnp.zeros((V_LORA,), F32)
        else:
            w_vd = vres_w_down[l - 1]
            mu_vd = vres_mu[l - 1]
        wr = jnp.concatenate([w_in[l][:, OFF_RWKV:OFF_GATE], w_vd, zr_pad], axis=1).astype(BF16)
        mu = jnp.concatenate([rwkv_mu[l], mu_vd, mu_pad]).reshape(1, ZR_COLS)
        wpc = w_in[l][:, :OFF_RWKV].astype(BF16)
        wg = w_in[l][:, OFF_GATE:OFF_GATE + 3 * d].astype(BF16)
        vres = None
        if l > 0:
            vres = (v_first, row(vres_v0[l - 1]), vres_v2[l - 1].astype(BF16))
        zpc, zg, r, lw, kh, v, kk, bb, g, bonus = _in_proj(
            x, mod, row(norm_g[l, 0]), wpc, wr, wg,
            mu, row(rwkv_w0[l]), rwkv_w2[l].astype(BF16), row(rwkv_a0[l]),
            rwkv_a2[l].astype(BF16), rwkv_g2[l].astype(BF16), row(rwkv_kk_scale[l]),
            row(rwkv_ka[l]), row(rwkv_rk[l]), vres)
        if l == 0:
            v_first = v
        o = _wkv_scan(r, lw, kh, v, kk, bb)
        x = _post(x, mod, zpc, zg, o, bonus, g,
                  _block_diag(pool_w[l]).astype(BF16), row(pool_scale[l]), pool_proj[l].astype(BF16),
                  conv_w[l], row(conv_b[l]), row(conv_ln_g[l]), row(conv_ln_b[l]),
                  conv_proj[l].astype(BF16), row(rwkv_lnx_g[l]), row(rwkv_lnx_b[l]),
                  rwkv_proj[l].astype(BF16), w_o[l].astype(BF16))
        mod = mods[2 * l + 1]
        gn = row(norm_g[l, 1])
        i = l // 2
        last = l == depth - 1
        if l % 2 == 0:
            x = _ffn(x, mod, gn, ffn_w1[i].astype(BF16), ffn_w3[i].astype(BF16), ffn_w2[i].astype(BF16))
            if last:
                x = _finish(x, g=row(final_norm_g))
        else:
            gate, h = _router(x, mod, gn, moe_router[i])
            n_e = gate.shape[2]
            y = _moe(h.reshape(-1, d), gate.reshape(-1, n_e), moe_w1[i].astype(BF16),
                     moe_w3[i].astype(BF16), moe_w2[i].astype(BF16)).reshape(x.shape)
            x = _finish(x, y, mod, row(final_norm_g) if last else None)
    return x
```

```python
import functools

import jax
import jax.numpy as jnp
from jax import lax
from jax.experimental import pallas as pl
from jax.experimental.pallas import tpu as pltpu

F32 = jnp.float32
BF16 = jnp.bfloat16

POOL_WINDOWS = (2, 4, 8, 16)
POOL_GROUP_W = 64
POOL_W = 256
CONV_W = 256
CONV_K = 31
RWKV_HEAD = 64
RWKV_W = 512
RWKV_HEADS = 8
W_LORA = 32
A_LORA = 32
V_LORA = 16
G_LORA = 64
RWKV_COLS = 3 * RWKV_W + W_LORA + A_LORA + G_LORA
ZR_COLS = 1792
OFF_CONV = POOL_W
OFF_RWKV = OFF_CONV + 2 * CONV_W
OFF_GATE = OFF_RWKV + RWKV_COLS
N_EXPERTS = 8
RMS_EPS = 1e-6
LN_EPS = 1e-5
GN_EPS = 64e-5

LANES = 128
SUBLANES = 8
VMEM_LIMIT = 56 * 1024 * 1024
CHUNK = 64
HALO = 32


def _sigmoid(x):
    return 1.0 / (1.0 + jnp.exp(-x))


def _silu(x):
    return x * _sigmoid(x)


def _dot(a, b):
    return jnp.dot(a, b, preferred_element_type=F32)


def _ada_rms(x, mod, g):
    y = x * lax.rsqrt(jnp.mean(x * x, axis=-1, keepdims=True) + RMS_EPS) * g
    return y * (1.0 + mod[1:2, :]) + mod[0:1, :]


def _params(sem):
    return pltpu.CompilerParams(dimension_semantics=sem, vmem_limit_bytes=VMEM_LIMIT)


def _mod_kernel(c_ref, w_ref, b_ref, o_ref):
    c = c_ref[...]
    o_ref[0] = jnp.dot(_silu(c), w_ref[0], preferred_element_type=F32,
                       precision=lax.Precision.HIGHEST) + b_ref[0]


def _ada_mods(c, ada_w, ada_b):
    b_, d = c.shape
    n_sub = ada_w.shape[0] * ada_w.shape[1]
    w = ada_w.reshape(n_sub, d, 3 * d)
    bias = ada_b.reshape(n_sub, 1, 3 * d)
    c8 = jnp.zeros((SUBLANES, d), F32).at[:b_].set(c)
    tn = 1024
    out = pl.pallas_call(
        _mod_kernel,
        grid=(n_sub, 3 * d // tn),
        in_specs=[pl.BlockSpec((SUBLANES, d), lambda s, j: (0, 0)),
                  pl.BlockSpec((1, d, tn), lambda s, j: (s, 0, j)),
                  pl.BlockSpec((1, 1, tn), lambda s, j: (s, 0, j))],
        out_specs=pl.BlockSpec((1, SUBLANES, tn), lambda s, j: (s, 0, j)),
        out_shape=jax.ShapeDtypeStruct((n_sub, SUBLANES, 3 * d), F32),
        compiler_params=_params(("parallel", "parallel")),
        name="ada_mod",
    )(c8, w, bias)
    return out[:, :b_].reshape(n_sub, b_, 3, d)


def _head_sum(x):
    low = lax.broadcasted_iota(jnp.int32, (1, LANES), 1) < RWKV_HEAD
    outs = []
    for cb in range(x.shape[1] // LANES):
        xc = x[:, cb * LANES:(cb + 1) * LANES]
        s_lo = jnp.sum(jnp.where(low, xc, 0.0), axis=-1, keepdims=True)
        s_hi = jnp.sum(jnp.where(low, 0.0, xc), axis=-1, keepdims=True)
        outs.append(jnp.where(low, s_lo, s_hi))
    return jnp.concatenate(outs, axis=-1)


def _inproj_kernel(has_vres, x_ref, mod_ref, gn_ref, wpc_ref, wr_ref, wg_ref, mu_ref, w0_ref, w2_ref,
                   a0_ref, a2_ref, g2_ref, ksc_ref, ka_ref, rk_ref, *rest):
    if has_vres:
        vf_ref, v0_ref, v2_ref = rest[:3]
        rest = rest[3:]
    zpc_ref, zg_ref, r_ref, lw_ref, k_ref, v_ref, kk_ref, b_ref, g_ref, bonus_ref, last_ref = rest

    i = pl.program_id(1)
    hb = _ada_rms(x_ref[0], mod_ref[0], gn_ref[...]).astype(BF16)
    z = _dot(hb, wr_ref[...])
    ts = z.shape[0]
    prev = jnp.where(i > 0, last_ref[SUBLANES - 1:SUBLANES, :], 0.0)
    last_ref[...] = z[ts - SUBLANES:, :]
    row = lax.broadcasted_iota(jnp.int32, (ts, 1), 0)
    zsh = jnp.where(row == 0, prev, pltpu.roll(z, shift=1, axis=0))
    z = z + (zsh - z) * mu_ref[...]

    w3 = RWKV_W
    r = z[:, 0:w3]
    k = z[:, w3:2 * w3]
    v = z[:, 2 * w3:3 * w3]
    o = 3 * w3
    wd = z[:, o:o + W_LORA]
    ad = z[:, o + W_LORA:o + W_LORA + A_LORA]
    gd = z[:, o + W_LORA + A_LORA:o + W_LORA + A_LORA + G_LORA]

    wpre = w0_ref[...] + _dot(jnp.tanh(wd).astype(BF16), w2_ref[...])
    y = -wpre
    softplus = jnp.maximum(y, 0.0) + jnp.log(1.0 + jnp.exp(-jnp.abs(y)))
    w = -softplus - 0.5
    lw_ref[0] = -jnp.exp(w)
    a = _sigmoid(a0_ref[...] + _dot(ad.astype(BF16), a2_ref[...]))
    g_ref[0] = _dot(_sigmoid(gd).astype(BF16), g2_ref[...])
    if has_vres:
        vd = z[:, RWKV_COLS:RWKV_COLS + V_LORA]
        mix = _sigmoid(v0_ref[...] + _dot(vd.astype(BF16), v2_ref[...]))
        v = v + (vf_ref[0] - v) * mix
    kk = k * ksc_ref[...]
    kk = kk / jnp.maximum(jnp.sqrt(_head_sum(kk * kk)), 1e-12)
    kh = k * (1.0 + (a - 1.0) * ka_ref[...])
    r_ref[0] = r
    k_ref[0] = kh
    v_ref[0] = v
    kk_ref[0] = kk
    b_ref[0] = kk * a
    bonus_ref[0] = _head_sum(r * kh * rk_ref[...]) * v
    zpc_ref[0] = _dot(hb, wpc_ref[...])
    zg_ref[0] = _sigmoid(_dot(hb, wg_ref[...])).astype(BF16)


def _in_proj(x, mod, gn, wpc, wr, wg, mu, w0, w2, a0, a2, g2, ksc, ka, rk, vres, ts=512):
    b_, s_, d = x.shape
    zc = wr.shape[1]
    has_vres = vres is not None
    const = lambda b, i: (0, 0)
    tile = lambda b, i: (b, i, 0)
    vec = pl.BlockSpec((1, RWKV_W), const)
    in_specs = [pl.BlockSpec((1, ts, d), tile),
                pl.BlockSpec((1, 3, d), lambda b, i: (b, 0, 0)),
                pl.BlockSpec((1, d), const),
                pl.BlockSpec(wpc.shape, const, pipeline_mode=pl.Buffered(1)),
                pl.BlockSpec(wr.shape, const, pipeline_mode=pl.Buffered(1)),
                pl.BlockSpec(wg.shape, const, pipeline_mode=pl.Buffered(1)),
                pl.BlockSpec((1, zc), const),
                vec, pl.BlockSpec((W_LORA, RWKV_W), const),
                vec, pl.BlockSpec((A_LORA, RWKV_W), const),
                pl.BlockSpec((G_LORA, RWKV_W), const),
                vec, vec, vec]
    args = [x, mod, gn, wpc, wr, wg, mu, w0, w2, a0, a2, g2, ksc, ka, rk]
    if has_vres:
        v_first, v0, v2 = vres
        in_specs += [pl.BlockSpec((1, ts, RWKV_W), tile), vec,
                     pl.BlockSpec((V_LORA, RWKV_W), const)]
        args += [v_first, v0, v2]
    out = jax.ShapeDtypeStruct((b_, s_, RWKV_W), F32)
    return pl.pallas_call(
        functools.partial(_inproj_kernel, has_vres),
        grid=(b_, s_ // ts),
        in_specs=in_specs,
        out_specs=[pl.BlockSpec((1, ts, wpc.shape[1]), tile),
                   pl.BlockSpec((1, ts, wg.shape[1]), tile)] + [pl.BlockSpec((1, ts, RWKV_W), tile)] * 8,
        out_shape=[jax.ShapeDtypeStruct((b_, s_, wpc.shape[1]), F32),
                   jax.ShapeDtypeStruct((b_, s_, wg.shape[1]), BF16)] + [out] * 8,
        scratch_shapes=[pltpu.VMEM((SUBLANES, zc), F32)],
        compiler_params=_params(("parallel", "arbitrary")),
        name="in_proj",
    )(*args)


def _split3(x):
    hi = x.astype(BF16)
    r1 = x - hi.astype(F32)
    mid = r1.astype(BF16)
    lo = (r1 - mid.astype(F32)).astype(BF16)
    return hi, mid, lo


def _dot_nt(a, b):
    return lax.dot_general(a, b, (((1,), (1,)), ((), ())), preferred_element_type=F32)


def _dot_tn(a, b):
    return lax.dot_general(a, b, (((0,), (0,)), ((), ())), preferred_element_type=F32)


CHUNKS_PER_STEP = 4


def _wkv_kernel(r_ref, lw_ref, k_ref, v_ref, kk_ref, b_ref, o_ref, z_ref):
    @pl.when(pl.program_id(1) == 0)
    def _():
        z_ref[...] = jnp.zeros_like(z_ref)

    c_ = CHUNK
    n_ = RWKV_HEAD
    nc = CHUNKS_PER_STEP
    n_pairs = RWKV_HEADS // 2
    n_steps = r_ref.shape[1] // (c_ * nc)
    ri = lax.broadcasted_iota(jnp.int32, (c_, c_), 0)
    ci = lax.broadcasted_iota(jnp.int32, (c_, c_), 1)
    tril_incl = (ri >= ci)
    tril_strict = (ri > ci)
    rs = lax.broadcasted_iota(jnp.int32, (nc * c_, nc * c_), 0)
    cs = lax.broadcasted_iota(jnp.int32, (nc * c_, nc * c_), 1)
    tri_b = jnp.where((rs >= cs) & ((rs // c_) == (cs // c_)), 1.0, 0.0).astype(BF16)
    rl = lax.broadcasted_iota(jnp.int32, (LANES, LANES), 0)
    cl = lax.broadcasted_iota(jnp.int32, (LANES, LANES), 1)
    eye_pair = (rl == cl)
    same_head = ((rl < n_) == (cl < n_))
    head0 = lax.broadcasted_iota(jnp.int32, (1, LANES), 1) < n_
    pl_ = lambda a, p: a[:, p * LANES:(p + 1) * LANES]
    ch_ = lambda a, q: a[q * c_:(q + 1) * c_]

    def step_body(st, carry):
        rows = pl.ds(pl.multiple_of(st * (nc * c_), nc * c_), nc * c_)
        lw = lw_ref[0, rows, :]
        hi, mid, lo = _split3(lw)
        cum = _dot(tri_b, hi) + _dot(tri_b, mid) + _dot(tri_b, lo)
        cum_last = jnp.concatenate(
            [jnp.broadcast_to(cum[(q + 1) * c_ - 1:(q + 1) * c_, :], (c_, cum.shape[1])) for q in range(nc)],
            axis=0)
        w_inv = jnp.exp(-cum)
        w_tail = jnp.exp(cum_last - cum)
        w_chunk = jnp.exp(cum_last)
        kk = kk_ref[0, rows, :]
        bb = b_ref[0, rows, :]
        kx = k_ref[0, rows, :]
        at32 = -kk * jnp.exp(cum - lw)
        rt32 = r_ref[0, rows, :] * jnp.exp(cum)
        at = at32.astype(BF16)
        rt = rt32.astype(BF16)
        bt = (bb * w_inv).astype(BF16)
        kt = (kx * w_inv).astype(BF16)
        bh = (bb * w_tail).astype(BF16)
        kh = (kx * w_tail).astype(BF16)
        v32 = v_ref[0, rows, :]
        vrot = jnp.concatenate([pltpu.roll(pl_(v32, p), shift=n_, axis=1) for p in range(n_pairs)],
                               axis=1).astype(BF16)

        chains = [(q, p, h) for q in range(nc) for p in range(n_pairs) for h in range(2)]
        aa = []
        for q, p, h in chains:
            lhs = jnp.concatenate([ch_(pl_(at, p), q), ch_(pl_(rt, p), q)], axis=0)
            rhs = jnp.concatenate([ch_(pl_(bt, p), q), ch_(pl_(kt, p), q)], axis=0)
            keep = head0 if h == 0 else jnp.logical_not(head0)
            aa.append(_dot_nt(jnp.where(keep, lhs, jnp.zeros_like(lhs)), rhs))
        a_ab = [jnp.where(tril_strict, a[:c_, :c_], 0.0).astype(BF16) for a in aa]
        a_rb = [jnp.where(tril_incl, a[c_:, :c_], 0.0).astype(BF16) for a in aa]
        a_k = [jnp.concatenate([jnp.where(tril_strict, a[:c_, c_:], 0.0),
                                jnp.where(tril_incl, a[c_:, c_:], 0.0)], axis=0).astype(BF16)
               for a in aa]
        av = [_dot(a_k[i], ch_(pl_(vrot, p), q)) for i, (q, p, h) in enumerate(chains)]
        own = [head0 if h == 0 else jnp.logical_not(head0) for q, p, h in chains]
        xs = [jnp.where(own[i], ch_(pl_(at32, p), q), av[i][:c_])
              for i, (q, p, h) in enumerate(chains)]
        ps = a_ab
        for it in range(6):
            xs = [x + _dot(pw, x.astype(BF16)) for x, pw in zip(xs, ps)]
            if it < 5:
                ps = [_dot(pw, pw).astype(BF16) for pw in ps]
        qos = [jnp.where(own[i], ch_(pl_(rt32, p), q), av[i][c_:]) + _dot(a_rb[i], xs[i].astype(BF16))
               for i, (q, p, h) in enumerate(chains)]
        ops = []
        for q in range(nc):
            for p in range(n_pairs):
                i0 = (q * n_pairs + p) * 2
                g_pair = jnp.where(head0, xs[i0], xs[i0 + 1])
                u_swap = jnp.where(head0, xs[i0 + 1], xs[i0])
                q_pair = jnp.where(head0, qos[i0], qos[i0 + 1])
                o_swap = jnp.where(head0, qos[i0 + 1], qos[i0])
                gu = jnp.concatenate([g_pair, u_swap], axis=1).astype(BF16)
                m1 = _dot_tn(ch_(pl_(bh, p), q), gu)
                m2 = _dot_tn(ch_(pl_(kh, p), q), ch_(pl_(vrot, p), q))
                wc = ch_(pl_(w_chunk, p), q)[0:1, :]
                phi_t = jnp.where(eye_pair, wc, 0.0) + jnp.where(same_head, m1[:, :LANES], 0.0)
                psi_t = jnp.where(same_head, 0.0, m1[:, LANES:] + m2)
                ops.append((q_pair.astype(BF16), o_swap, phi_t.astype(BF16), psi_t))
        for p in range(n_pairs):
            z = z_ref[p]
            outs = []
            for q in range(nc):
                q_pair, o_swap, phi_t, psi_t = ops[q * n_pairs + p]
                zb = z.astype(BF16)
                outs.append(pltpu.roll(_dot(q_pair, zb) + o_swap, shift=n_, axis=1))
                z = _dot(phi_t, zb) + psi_t
            z_ref[p] = z
            o_ref[0, rows, p * LANES:(p + 1) * LANES] = jnp.concatenate(outs, axis=0)
        return carry

    lax.fori_loop(0, n_steps, step_body, 0)


def _wkv_scan(r, lw, k, v, kk, bb, tc=512):
    b_, s_, w_ = r.shape
    tile = lambda b, i: (b, i, 0)
    spec = pl.BlockSpec((1, tc, w_), tile)
    return pl.pallas_call(
        _wkv_kernel,
        grid=(b_, s_ // tc),
        in_specs=[spec] * 6,
        out_specs=spec,
        out_shape=jax.ShapeDtypeStruct((b_, s_, w_), F32),
        scratch_shapes=[pltpu.VMEM((RWKV_HEADS // 2, LANES, LANES), F32)],
        compiler_params=_params(("parallel", "arbitrary")),
        name="wkv_scan",
    )(r, lw, k, v, kk, bb)


def _post_kernel(x_ref, mod_ref, zpc_ref, halo_ref, zg_ref, o_ref, bonus_ref, g_ref,
                 poolw_ref, pscale_ref, pproj_ref, convw_ref, convb_ref, lng_ref, lnb_ref,
                 cproj_ref, lnxg_ref, lnxb_ref, rproj_ref, wo_ref, out_ref, ext_ref):
    i = pl.program_id(1)
    ts = x_ref.shape[1]
    halo = jnp.where(i > 0, halo_ref[0], 0.0)
    cur = zpc_ref[0]

    up = cur[:, :POOL_W]
    ext = jnp.concatenate([halo[:, :POOL_W], up], axis=0)
    e1 = ext + pltpu.roll(ext, shift=1, axis=0)
    e2 = e1 + pltpu.roll(e1, shift=2, axis=0)
    e3 = e2 + pltpu.roll(e2, shift=4, axis=0)
    e4 = e3 + pltpu.roll(e3, shift=8, axis=0)
    lane = lax.broadcasted_iota(jnp.int32, (1, POOL_W), 1)
    grp = lane // POOL_GROUP_W
    pooled = jnp.where(grp == 0, e1, jnp.where(grp == 1, e2, jnp.where(grp == 2, e3, e4)))[HALO:]
    win = jnp.where(grp == 0, 2.0, jnp.where(grp == 1, 4.0, jnp.where(grp == 2, 8.0, 16.0)))
    pos = (i * ts + 1 + lax.broadcasted_iota(jnp.int32, (ts, 1), 0)).astype(F32)
    p = pooled / jnp.minimum(pos, win) - up
    p = _dot(p.astype(BF16), poolw_ref[...]) * pscale_ref[...]
    y_pool = _dot(p.astype(BF16), pproj_ref[...])

    za = jnp.concatenate([halo[:, POOL_W:POOL_W + CONV_W], cur[:, POOL_W:POOL_W + CONV_W]], axis=0)
    zb = jnp.concatenate([halo[:, POOL_W + CONV_W:], cur[:, POOL_W + CONV_W:]], axis=0)
    u = za * _sigmoid(zb)
    ext_ref[0] = u
    for r in range(1, SUBLANES):
        ext_ref[r] = pltpu.roll(u, shift=u.shape[0] - r, axis=0)
    acc = jnp.zeros((ts, CONV_W), F32) + convb_ref[...]
    for j in range(CONV_K):
        st = HALO - (CONV_K - 1) + j
        al = st - st % SUBLANES
        acc = acc + ext_ref[st % SUBLANES, al:al + ts, :] * convw_ref[j:j + 1, :]
    mean = jnp.mean(acc, axis=-1, keepdims=True)
    dev = acc - mean
    var = jnp.mean(dev * dev, axis=-1, keepdims=True)
    un = dev * lax.rsqrt(var + LN_EPS) * lng_ref[...] + lnb_ref[...]
    y_conv = _dot(_silu(un).astype(BF16), cproj_ref[...])

    o = o_ref[0]
    inv_n = 1.0 / RWKV_HEAD
    mu = _head_sum(o) * inv_n
    od = o - mu
    ovar = _head_sum(od * od) * inv_n
    on = od * lax.rsqrt(ovar + GN_EPS) * lnxg_ref[...] + lnxb_ref[...]
    y_rwkv = _dot(((on + bonus_ref[0]) * g_ref[0]).astype(BF16), rproj_ref[...])

    d = x_ref.shape[2]
    zg = zg_ref[0]
    merged = (zg[:, :d].astype(F32) * y_pool + zg[:, d:2 * d].astype(F32) * y_conv
              + zg[:, 2 * d:].astype(F32) * y_rwkv)
    out_ref[0] = x_ref[0] + mod_ref[0, 2:3, :] * _dot(merged.astype(BF16), wo_ref[...])


def _post(x, mod, zpc, zg, o, bonus, g, poolw, pscale, pproj, convw, convb, lng, lnb, cproj,
          lnxg, lnxb, rproj, wo, ts=256):
    b_, s_, d = x.shape
    const = lambda b, i: (0, 0)
    tile = lambda b, i: (b, i, 0)
    full = lambda a: pl.BlockSpec(a.shape, const)
    in_specs = [pl.BlockSpec((1, ts, d), tile),
                pl.BlockSpec((1, 3, d), lambda b, i: (b, 0, 0)),
                pl.BlockSpec((1, ts, zpc.shape[2]), tile),
                pl.BlockSpec((1, HALO, zpc.shape[2]),
                             lambda b, i: (b, jnp.maximum(i * (ts // HALO) - 1, 0), 0)),
                pl.BlockSpec((1, ts, zg.shape[2]), tile),
                pl.BlockSpec((1, ts, RWKV_W), tile),
                pl.BlockSpec((1, ts, RWKV_W), tile),
                pl.BlockSpec((1, ts, RWKV_W), tile)]
    weights = [poolw, pscale, pproj, convw, convb, lng, lnb, cproj, lnxg, lnxb, rproj, wo]
    in_specs += [full(a) for a in weights]
    return pl.pallas_call(
        _post_kernel,
        grid=(b_, s_ // ts),
        in_specs=in_specs,
        out_specs=pl.BlockSpec((1, ts, d), tile),
        out_shape=jax.ShapeDtypeStruct((b_, s_, d), F32),
        scratch_shapes=[pltpu.VMEM((SUBLANES, HALO + ts, CONV_W), F32)],
        compiler_params=_params(("parallel", "parallel")),
        name="mix_merge",
    )(x, mod, zpc, zpc, zg, o, bonus, g, *weights)


def _router_kernel(x_ref, mod_ref, g_ref, wr_ref, gate_ref, h_ref):
    h = _ada_rms(x_ref[0], mod_ref[0], g_ref[...])
    h_ref[0] = h.astype(BF16)
    logits = jnp.dot(h, wr_ref[...], preferred_element_type=F32, precision=lax.Precision.HIGHEST)
    n_e = logits.shape[1]
    lane = lax.broadcasted_iota(jnp.int32, logits.shape, 1)
    m1 = jnp.max(logits, axis=-1, keepdims=True)
    i1 = jnp.min(jnp.where(logits == m1, lane, n_e), axis=-1, keepdims=True)
    sel1 = lane == i1
    rest = jnp.where(sel1, -jnp.inf, logits)
    m2 = jnp.max(rest, axis=-1, keepdims=True)
    i2 = jnp.min(jnp.where(rest == m2, lane, n_e), axis=-1, keepdims=True)
    sel2 = lane == i2
    e2 = jnp.exp(m2 - m1)
    w1 = 1.0 / (1.0 + e2)
    gate_ref[0] = jnp.where(sel1, w1, 0.0) + jnp.where(sel2, e2 * w1, 0.0)


def _router(x, mod, g, wr, ts=512):
    b_, s_, d = x.shape
    n_e = wr.shape[1]
    tile = lambda b, i: (b, i, 0)
    return pl.pallas_call(
        _router_kernel,
        grid=(b_, s_ // ts),
        in_specs=[pl.BlockSpec((1, ts, d), tile),
                  pl.BlockSpec((1, 3, d), lambda b, i: (b, 0, 0)),
                  pl.BlockSpec((1, d), lambda b, i: (0, 0)),
                  pl.BlockSpec(wr.shape, lambda b, i: (0, 0))],
        out_specs=[pl.BlockSpec((1, ts, n_e), tile), pl.BlockSpec((1, ts, d), tile)],
        out_shape=[jax.ShapeDtypeStruct((b_, s_, n_e), F32), jax.ShapeDtypeStruct((b_, s_, d), BF16)],
        compiler_params=_params(("parallel", "parallel")),
        name="moe_router",
    )(x, mod, g, wr)


PIECE = 128
SUB = 256


def _moe_kernel(cnt_ref, lo_ref, hi_ref, h_ref, rank_em_ref, rank_tm_ref, gate_ref,
                w1_ref, w3_ref, w2_ref, y_ref, xs_ref, acc_ref, tmp_ref):
    b = pl.program_id(0)
    e = pl.program_id(1)
    f = pl.program_id(2)
    n_e = pl.num_programs(1)
    max_pieces = h_ref.shape[0] // PIECE
    pair = b * n_e + e
    n_pieces = (cnt_ref[pair] + (PIECE - 1)) // PIECE
    n_big = n_pieces // 2
    has_tail = (n_pieces % 2) == 1
    sub_rows = lambda s: pl.ds(pl.multiple_of(s * SUB, SUB), SUB)

    def window_bounds(j0, m):
        return lo_ref[pair * max_pieces + j0], hi_ref[pair * max_pieces + j0 + m // PIECE - 1] + 1

    def for_groups(fn):
        def big(i, carry):
            fn(2 * i, 2 * PIECE)
            return carry
        lax.fori_loop(0, n_big, big, 0)

        @pl.when(has_tail)
        def _():
            fn(2 * n_big, PIECE)

    def rows_of(j0, m):
        return pl.ds(pl.multiple_of(j0 * PIECE, PIECE), m)

    @pl.when((e == 0) & (f == 0))
    def _():
        y_ref[...] = jnp.zeros_like(y_ref)

    def gather(j0, m):
        slot = lax.broadcasted_iota(jnp.int32, (m, 1), 0) + j0 * PIECE
        tmp_ref[0:m, :] = jnp.zeros((m, tmp_ref.shape[1]), F32)

        lo, hi = window_bounds(j0, m)

        def window_pair(i, carry):
            s0 = lo + 2 * i
            s1 = jnp.minimum(s0 + 1, hi - 1)
            hit1 = (rank_em_ref[0, 0, s1] == slot) & (s0 + 1 < hi)
            onehot0 = jnp.where(rank_em_ref[0, 0, s0] == slot, 1.0, 0.0).astype(BF16)
            onehot1 = jnp.where(hit1, 1.0, 0.0).astype(BF16)
            tmp_ref[0:m, :] += (_dot(onehot0, h_ref[sub_rows(s0), :])
                                + _dot(onehot1, h_ref[sub_rows(s1), :]))
            return carry
        lax.fori_loop(0, (hi - lo + 1) // 2, window_pair, 0)
        xs_ref[rows_of(j0, m), :] = tmp_ref[0:m, :].astype(BF16)

    @pl.when(f == 0)
    def _():
        for_groups(gather)

    def expert(j0, m):
        rows = rows_of(j0, m)
        xj = xs_ref[rows, :]
        act = _silu(_dot(xj, w1_ref[0])) * _dot(xj, w3_ref[0])
        part = _dot(act.astype(BF16), w2_ref[0])

        @pl.when(f == 0)
        def _():
            acc_ref[rows, :] = part

        @pl.when(f > 0)
        def _():
            acc_ref[rows, :] += part
    for_groups(expert)

    def scatter(j0, m):
        lane_e = lax.broadcasted_iota(jnp.int32, (1, n_e), 1) == e
        slot = lax.broadcasted_iota(jnp.int32, (1, m), 1) + j0 * PIECE
        yj = acc_ref[rows_of(j0, m), :].astype(BF16)

        lo, hi = window_bounds(j0, m)

        def contribution(s, live):
            rows = sub_rows(s)
            rank_col = jnp.sum(jnp.where(lane_e, rank_tm_ref[rows, :], 0), axis=-1, keepdims=True)
            gate_col = jnp.sum(jnp.where(lane_e, gate_ref[rows, :], 0.0), axis=-1, keepdims=True)
            onehot_t = jnp.where((rank_col == slot) & live, 1.0, 0.0).astype(BF16)
            return gate_col * _dot(onehot_t, yj)

        def window_pair(i, carry):
            s0 = lo + 2 * i
            s1 = jnp.minimum(s0 + 1, hi - 1)
            c0 = contribution(s0, True)
            c1 = contribution(s1, s0 + 1 < hi)
            y_ref[sub_rows(s0), :] += c0
            y_ref[sub_rows(s1), :] += c1
            return carry
        lax.fori_loop(0, (hi - lo + 1) // 2, window_pair, 0)

    @pl.when(f == pl.num_programs(2) - 1)
    def _():
        for_groups(scatter)


def _moe(h, gate, w1, w3, w2, tb=2048, tf=1792):
    t_, d = h.shape
    n_e, _, ff = w1.shape
    nb = t_ // tb
    n_sub = tb // SUB
    max_pieces = tb // PIECE
    sel = (gate != 0.0).reshape(nb, tb, n_e).astype(jnp.int32)
    rank_tm = jnp.where(sel > 0, jnp.cumsum(sel, axis=1) - 1, -1)
    ends = jnp.cumsum(jnp.sum(sel.reshape(nb, n_sub, SUB, n_e), axis=2), axis=1)
    ends = jnp.transpose(ends, (0, 2, 1))
    cnt = ends[:, :, -1]
    starts = jnp.concatenate([jnp.zeros_like(ends[:, :, :1]), ends[:, :, :-1]], axis=2)
    first = (jnp.arange(max_pieces, dtype=jnp.int32) * PIECE)[None, None, :, None]
    last = jnp.minimum(first + PIECE, cnt[:, :, None, None])
    win_lo = jnp.sum((ends[:, :, None, :] <= first).astype(jnp.int32), axis=-1)
    win_hi = jnp.sum((starts[:, :, None, :] < last).astype(jnp.int32), axis=-1) - 1
    rank_em = jnp.transpose(rank_tm, (0, 2, 1)).reshape(nb, n_e, n_sub, 1, SUB)
    once = pl.Buffered(1)
    grid_spec = pltpu.PrefetchScalarGridSpec(
        num_scalar_prefetch=3,
        grid=(nb, n_e, ff // tf),
        in_specs=[pl.BlockSpec((tb, d), lambda b, e, f, *_: (b, 0), pipeline_mode=once),
                  pl.BlockSpec((1, 1, n_sub, 1, SUB), lambda b, e, f, *_: (b, e, 0, 0, 0)),
                  pl.BlockSpec((tb, n_e), lambda b, e, f, *_: (b, 0), pipeline_mode=once),
                  pl.BlockSpec((tb, n_e), lambda b, e, f, *_: (b, 0), pipeline_mode=once),
                  pl.BlockSpec((1, d, tf), lambda b, e, f, *_: (e, 0, f)),
                  pl.BlockSpec((1, d, tf), lambda b, e, f, *_: (e, 0, f)),
                  pl.BlockSpec((1, tf, d), lambda b, e, f, *_: (e, f, 0))],
        out_specs=pl.BlockSpec((tb, d), lambda b, e, f, *_: (b, 0), pipeline_mode=once),
        scratch_shapes=[pltpu.VMEM((tb, d), BF16), pltpu.VMEM((tb, d), F32),
                        pltpu.VMEM((2 * PIECE, d), F32)])
    return pl.pallas_call(
        _moe_kernel,
        grid_spec=grid_spec,
        out_shape=jax.ShapeDtypeStruct((t_, d), F32),
        compiler_params=_params(("parallel", "arbitrary", "arbitrary")),
        name="moe_sparse",
    )(cnt.reshape(-1), win_lo.reshape(-1), win_hi.reshape(-1), h, rank_em,
      rank_tm.reshape(t_, n_e), gate, w1, w3, w2)


def _ffn_kernel(x_ref, mod_ref, g_ref, w1_ref, w3_ref, w2_ref, out_ref):
    x = x_ref[0]
    h = _ada_rms(x, mod_ref[0], g_ref[...]).astype(BF16)
    act = _silu(_dot(h, w1_ref[...])) * _dot(h, w3_ref[...])
    out_ref[0] = x + mod_ref[0, 2:3, :] * _dot(act.astype(BF16), w2_ref[...])


def _ffn(x, mod, g, w1, w3, w2, tm=256):
    b_, s_, d = x.shape
    tile = lambda b, i: (b, i, 0)
    const = lambda b, i: (0, 0)
    once = pl.Buffered(1)
    return pl.pallas_call(
        _ffn_kernel,
        grid=(b_, s_ // tm),
        in_specs=[pl.BlockSpec((1, tm, d), tile),
                  pl.BlockSpec((1, 3, d), lambda b, i: (b, 0, 0)),
                  pl.BlockSpec((1, d), const),
                  pl.BlockSpec(w1.shape, const, pipeline_mode=once),
                  pl.BlockSpec(w3.shape, const, pipeline_mode=once),
                  pl.BlockSpec(w2.shape, const, pipeline_mode=once)],
        out_specs=pl.BlockSpec((1, tm, d), tile),
        out_shape=jax.ShapeDtypeStruct((b_, s_, d), F32),
        compiler_params=_params(("parallel", "parallel")),
        name="ffn",
    )(x, mod, g, w1, w3, w2)


def _finish_kernel(has_y, has_norm, x_ref, *rest):
    x = x_ref[0]
    if has_y:
        y_ref, mod_ref = rest[:2]
        rest = rest[2:]
        x = x + mod_ref[0, 2:3, :] * y_ref[0]
    if has_norm:
        g_ref = rest[0]
        rest = rest[1:]
        x = x * lax.rsqrt(jnp.mean(x * x, axis=-1, keepdims=True) + RMS_EPS) * g_ref[...]
    rest[0][0] = x


def _finish(x, y=None, mod=None, g=None, ts=512):
    b_, s_, d = x.shape
    tile = lambda b, i: (b, i, 0)
    in_specs = [pl.BlockSpec((1, ts, d), tile)]
    args = [x]
    if y is not None:
        in_specs += [pl.BlockSpec((1, ts, d), tile), pl.BlockSpec((1, 3, d), lambda b, i: (b, 0, 0))]
        args += [y, mod]
    if g is not None:
        in_specs.append(pl.BlockSpec((1, d), lambda b, i: (0, 0)))
        args.append(g)
    return pl.pallas_call(
        functools.partial(_finish_kernel, y is not None, g is not None),
        grid=(b_, s_ // ts),
        in_specs=in_specs,
        out_specs=pl.BlockSpec((1, ts, d), tile),
        out_shape=jax.ShapeDtypeStruct((b_, s_, d), F32),
        compiler_params=_params(("parallel", "parallel")),
        name="finish",
    )(*args)


def _block_diag(w):
    g_, c_, _ = w.shape
    out = jnp.zeros((g_ * c_, g_ * c_), w.dtype)
    for gi in range(g_):
        out = out.at[gi * c_:(gi + 1) * c_, gi * c_:(gi + 1) * c_].set(w[gi])
    return out


def kernel(x, c, ada_w, ada_b, norm_g, w_in, pool_w, pool_scale, pool_proj, conv_w, conv_b, conv_ln_g, conv_ln_b, conv_proj, rwkv_mu, rwkv_w0, rwkv_w2, rwkv_a0, rwkv_a2, rwkv_g2, rwkv_kk_scale, rwkv_ka, rwkv_rk, rwkv_lnx_g, rwkv_lnx_b, rwkv_proj, vres_w_down, vres_mu, vres_v0, vres_v2, w_o, ffn_w1, ffn_w3, ffn_w2, moe_router, moe_w1, moe_w3, moe_w2, final_norm_g):
    depth = w_in.shape[0]
    d = x.shape[2]
    row = lambda a: a.reshape(1, -1)
    mods = _ada_mods(c, ada_w, ada_b)
    v_first = None
    for l in range(depth):
        mod = mods[2 * l]
        zr_pad = jnp.zeros((d, ZR_COLS - RWKV_COLS - V_LORA), F32)
        mu_pad = jnp.zeros((ZR_COLS - RWKV_COLS - V_LORA,), F32)
        if l == 0:
            w_vd = jnp.zeros((d, V_LORA), F32)
            mu_vd = jnp.zeros((V_LORA,), F32)
        else:
            w_vd = vres_w_down[l - 1]
            mu_vd = vres_mu[l - 1]
        wr = jnp.concatenate([w_in[l][:, OFF_RWKV:OFF_GATE], w_vd, zr_pad], axis=1).astype(BF16)
        mu = jnp.concatenate([rwkv_mu[l], mu_vd, mu_pad]).reshape(1, ZR_COLS)
        wpc = w_in[l][:, :OFF_RWKV].astype(BF16)
        wg = w_in[l][:, OFF_GATE:OFF_GATE + 3 * d].astype(BF16)
        vres = None
        if l > 0:
            vres = (v_first, row(vres_v0[l - 1]), vres_v2[l - 1].astype(BF16))
        zpc, zg, r, lw, kh, v, kk, bb, g, bonus = _in_proj(
            x, mod, row(norm_g[l, 0]), wpc, wr, wg,
            mu, row(rwkv_w0[l]), rwkv_w2[l].astype(BF16), row(rwkv_a0[l]),
            rwkv_a2[l].astype(BF16), rwkv_g2[l].astype(BF16), row(rwkv_kk_scale[l]),
            row(rwkv_ka[l]), row(rwkv_rk[l]), vres)
        if l == 0:
            v_first = v
        o = _wkv_scan(r, lw, kh, v, kk, bb)
        x = _post(x, mod, zpc, zg, o, bonus, g,
                  _block_diag(pool_w[l]).astype(BF16), row(pool_scale[l]), pool_proj[l].astype(BF16),
                  conv_w[l], row(conv_b[l]), row(conv_ln_g[l]), row(conv_ln_b[l]),
                  conv_proj[l].astype(BF16), row(rwkv_lnx_g[l]), row(rwkv_lnx_b[l]),
                  rwkv_proj[l].astype(BF16), w_o[l].astype(BF16))
        mod = mods[2 * l + 1]
        gn = row(norm_g[l, 1])
        i = l // 2
        last = l == depth - 1
        if l % 2 == 0:
            x = _ffn(x, mod, gn, ffn_w1[i].astype(BF16), ffn_w3[i].astype(BF16), ffn_w2[i].astype(BF16))
            if last:
                x = _finish(x, g=row(final_norm_g))
        else:
            gate, h = _router(x, mod, gn, moe_router[i])
            n_e = gate.shape[2]
            y = _moe(h.reshape(-1, d), gate.reshape(-1, n_e), moe_w1[i].astype(BF16),
                     moe_w3[i].astype(BF16), moe_w2[i].astype(BF16)).reshape(x.shape)
            x = _finish(x, y, mod, row(final_norm_g) if last else None)
    return x
```

```python
import functools

import jax
import jax.numpy as jnp
from jax import lax
from jax.experimental import pallas as pl
from jax.experimental.pallas import tpu as pltpu

F32 = jnp.float32
BF16 = jnp.bfloat16

POOL_WINDOWS = (2, 4, 8, 16)
POOL_GROUP_W = 64
POOL_W = 256
CONV_W = 256
CONV_K = 31
RWKV_HEAD = 64
RWKV_W = 512
RWKV_HEADS = 8
W_LORA = 32
A_LORA = 32
V_LORA = 16
G_LORA = 64
RWKV_COLS = 3 * RWKV_W + W_LORA + A_LORA + G_LORA
ZR_COLS = 1792
OFF_CONV = POOL_W
OFF_RWKV = OFF_CONV + 2 * CONV_W
OFF_GATE = OFF_RWKV + RWKV_COLS
N_EXPERTS = 8
RMS_EPS = 1e-6
LN_EPS = 1e-5
GN_EPS = 64e-5

LANES = 128
SUBLANES = 8
VMEM_LIMIT = 56 * 1024 * 1024
CHUNK = 64
HALO = 32


def _sigmoid(x):
    return 1.0 / (1.0 + jnp.exp(-x))


def _silu(x):
    return x * _sigmoid(x)


def _dot(a, b):
    return jnp.dot(a, b, preferred_element_type=F32)


def _ada_rms(x, mod, g):
    y = x * lax.rsqrt(jnp.mean(x * x, axis=-1, keepdims=True) + RMS_EPS) * g
    return y * (1.0 + mod[1:2, :]) + mod[0:1, :]


def _params(sem):
    return pltpu.CompilerParams(dimension_semantics=sem, vmem_limit_bytes=VMEM_LIMIT)


def _mod_kernel(c_ref, w_ref, b_ref, o_ref):
    c = c_ref[...]
    o_ref[0] = jnp.dot(_silu(c), w_ref[0], preferred_element_type=F32,
                       precision=lax.Precision.HIGHEST) + b_ref[0]


def _ada_mods(c, ada_w, ada_b):
    b_, d = c.shape
    n_sub = ada_w.shape[0] * ada_w.shape[1]
    w = ada_w.reshape(n_sub, d, 3 * d)
    bias = ada_b.reshape(n_sub, 1, 3 * d)
    c8 = jnp.zeros((SUBLANES, d), F32).at[:b_].set(c)
    tn = 1024
    out = pl.pallas_call(
        _mod_kernel,
        grid=(n_sub, 3 * d // tn),
        in_specs=[pl.BlockSpec((SUBLANES, d), lambda s, j: (0, 0)),
                  pl.BlockSpec((1, d, tn), lambda s, j: (s, 0, j)),
                  pl.BlockSpec((1, 1, tn), lambda s, j: (s, 0, j))],
        out_specs=pl.BlockSpec((1, SUBLANES, tn), lambda s, j: (s, 0, j)),
        out_shape=jax.ShapeDtypeStruct((n_sub, SUBLANES, 3 * d), F32),
        compiler_params=_params(("parallel", "parallel")),
        name="ada_mod",
    )(c8, w, bias)
    return out[:, :b_].reshape(n_sub, b_, 3, d)


def _head_sum(x):
    low = lax.broadcasted_iota(jnp.int32, (1, LANES), 1) < RWKV_HEAD
    outs = []
    for cb in range(x.shape[1] // LANES):
        xc = x[:, cb * LANES:(cb + 1) * LANES]
        s_lo = jnp.sum(jnp.where(low, xc, 0.0), axis=-1, keepdims=True)
        s_hi = jnp.sum(jnp.where(low, 0.0, xc), axis=-1, keepdims=True)
        outs.append(jnp.where(low, s_lo, s_hi))
    return jnp.concatenate(outs, axis=-1)


def _inproj_kernel(has_vres, x_ref, mod_ref, gn_ref, wpc_ref, wr_ref, wg_ref, mu_ref, w0_ref, w2_ref,
                   a0_ref, a2_ref, g2_ref, ksc_ref, ka_ref, rk_ref, *rest):
    if has_vres:
        vf_ref, v0_ref, v2_ref = rest[:3]
        rest = rest[3:]
    zpc_ref, zg_ref, r_ref, lw_ref, k_ref, v_ref, kk_ref, b_ref, g_ref, bonus_ref, last_ref = rest

    i = pl.program_id(1)
    hb = _ada_rms(x_ref[0], mod_ref[0], gn_ref[...]).astype(BF16)
    z = _dot(hb, wr_ref[...])
    ts = z.shape[0]
    prev = jnp.where(i > 0, last_ref[SUBLANES - 1:SUBLANES, :], 0.0)
    last_ref[...] = z[ts - SUBLANES:, :]
    row = lax.broadcasted_iota(jnp.int32, (ts, 1), 0)
    zsh = jnp.where(row == 0, prev, pltpu.roll(z, shift=1, axis=0))
    z = z + (zsh - z) * mu_ref[...]

    w3 = RWKV_W
    r = z[:, 0:w3]
    k = z[:, w3:2 * w3]
    v = z[:, 2 * w3:3 * w3]
    o = 3 * w3
    wd = z[:, o:o + W_LORA]
    ad = z[:, o + W_LORA:o + W_LORA + A_LORA]
    gd = z[:, o + W_LORA + A_LORA:o + W_LORA + A_LORA + G_LORA]

    wpre = w0_ref[...] + _dot(jnp.tanh(wd).astype(BF16), w2_ref[...])
    y = -wpre
    softplus = jnp.maximum(y, 0.0) + jnp.log(1.0 + jnp.exp(-jnp.abs(y)))
    w = -softplus - 0.5
    lw_ref[0] = -jnp.exp(w)
    a = _sigmoid(a0_ref[...] + _dot(ad.astype(BF16), a2_ref[...]))
    g_ref[0] = _dot(_sigmoid(gd).astype(BF16), g2_ref[...])
    if has_vres:
        vd = z[:, RWKV_COLS:RWKV_COLS + V_LORA]
        mix = _sigmoid(v0_ref[...] + _dot(vd.astype(BF16), v2_ref[...]))
        v = v + (vf_ref[0] - v) * mix
    kk = k * ksc_ref[...]
    kk = kk / jnp.maximum(jnp.sqrt(_head_sum(kk * kk)), 1e-12)
    kh = k * (1.0 + (a - 1.0) * ka_ref[...])
    r_ref[0] = r
    k_ref[0] = kh
    v_ref[0] = v
    kk_ref[0] = kk
    b_ref[0] = kk * a
    bonus_ref[0] = _head_sum(r * kh * rk_ref[...]) * v
    zpc_ref[0] = _dot(hb, wpc_ref[...])
    zg_ref[0] = _sigmoid(_dot(hb, wg_ref[...])).astype(BF16)


def _in_proj(x, mod, gn, wpc, wr, wg, mu, w0, w2, a0, a2, g2, ksc, ka, rk, vres, ts=512):
    b_, s_, d = x.shape
    zc = wr.shape[1]
    has_vres = vres is not None
    const = lambda b, i: (0, 0)
    tile = lambda b, i: (b, i, 0)
    vec = pl.BlockSpec((1, RWKV_W), const)
    in_specs = [pl.BlockSpec((1, ts, d), tile),
                pl.BlockSpec((1, 3, d), lambda b, i: (b, 0, 0)),
                pl.BlockSpec((1, d), const),
                pl.BlockSpec(wpc.shape, const, pipeline_mode=pl.Buffered(1)),
                pl.BlockSpec(wr.shape, const, pipeline_mode=pl.Buffered(1)),
                pl.BlockSpec(wg.shape, const, pipeline_mode=pl.Buffered(1)),
                pl.BlockSpec((1, zc), const),
                vec, pl.BlockSpec((W_LORA, RWKV_W), const),
                vec, pl.BlockSpec((A_LORA, RWKV_W), const),
                pl.BlockSpec((G_LORA, RWKV_W), const),
                vec, vec, vec]
    args = [x, mod, gn, wpc, wr, wg, mu, w0, w2, a0, a2, g2, ksc, ka, rk]
    if has_vres:
        v_first, v0, v2 = vres
        in_specs += [pl.BlockSpec((1, ts, RWKV_W), tile), vec,
                     pl.BlockSpec((V_LORA, RWKV_W), const)]
        args += [v_first, v0, v2]
    out = jax.ShapeDtypeStruct((b_, s_, RWKV_W), F32)
    return pl.pallas_call(
        functools.partial(_inproj_kernel, has_vres),
        grid=(b_, s_ // ts),
        in_specs=in_specs,
        out_specs=[pl.BlockSpec((1, ts, wpc.shape[1]), tile),
                   pl.BlockSpec((1, ts, wg.shape[1]), tile)] + [pl.BlockSpec((1, ts, RWKV_W), tile)] * 8,
        out_shape=[jax.ShapeDtypeStruct((b_, s_, wpc.shape[1]), F32),
                   jax.ShapeDtypeStruct((b_, s_, wg.shape[1]), BF16)] + [out] * 8,
        scratch_shapes=[pltpu.VMEM((SUBLANES, zc), F32)],
        compiler_params=_params(("parallel", "arbitrary")),
        name="in_proj",
    )(*args)


def _split3(x):
    hi = x.astype(BF16)
    r1 = x - hi.astype(F32)
    mid = r1.astype(BF16)
    lo = (r1 - mid.astype(F32)).astype(BF16)
    return hi, mid, lo


def _dot_nt(a, b):
    return lax.dot_general(a, b, (((1,), (1,)), ((), ())), preferred_element_type=F32)


def _dot_tn(a, b):
    return lax.dot_general(a, b, (((0,), (0,)), ((), ())), preferred_element_type=F32)


CHUNKS_PER_STEP = 4


def _wkv_kernel(r_ref, lw_ref, k_ref, v_ref, kk_ref, b_ref, o_ref, z_ref):
    @pl.when(pl.program_id(1) == 0)
    def _():
        z_ref[...] = jnp.zeros_like(z_ref)

    c_ = CHUNK
    n_ = RWKV_HEAD
    nc = CHUNKS_PER_STEP
    n_pairs = RWKV_HEADS // 2
    n_steps = r_ref.shape[1] // (c_ * nc)
    ri = lax.broadcasted_iota(jnp.int32, (c_, c_), 0)
    ci = lax.broadcasted_iota(jnp.int32, (c_, c_), 1)
    tril_incl = (ri >= ci)
    tril_strict = (ri > ci)
    rs = lax.broadcasted_iota(jnp.int32, (nc * c_, nc * c_), 0)
    cs = lax.broadcasted_iota(jnp.int32, (nc * c_, nc * c_), 1)
    tri_b = jnp.where((rs >= cs) & ((rs // c_) == (cs // c_)), 1.0, 0.0).astype(BF16)
    rl = lax.broadcasted_iota(jnp.int32, (LANES, LANES), 0)
    cl = lax.broadcasted_iota(jnp.int32, (LANES, LANES), 1)
    eye_pair = (rl == cl)
    same_head = ((rl < n_) == (cl < n_))
    head0 = lax.broadcasted_iota(jnp.int32, (1, LANES), 1) < n_
    pl_ = lambda a, p: a[:, p * LANES:(p + 1) * LANES]
    ch_ = lambda a, q: a[q * c_:(q + 1) * c_]

    def step_body(st, carry):
        rows = pl.ds(pl.multiple_of(st * (nc * c_), nc * c_), nc * c_)
        lw = lw_ref[0, rows, :]
        hi, mid, lo = _split3(lw)
        cum = _dot(tri_b, hi) + _dot(tri_b, mid) + _dot(tri_b, lo)
        cum_last = jnp.concatenate(
            [jnp.broadcast_to(cum[(q + 1) * c_ - 1:(q + 1) * c_, :], (c_, cum.shape[1])) for q in range(nc)],
            axis=0)
        w_inv = jnp.exp(-cum)
        w_tail = jnp.exp(cum_last - cum)
        w_chunk = jnp.exp(cum_last)
        kk = kk_ref[0, rows, :]
        bb = b_ref[0, rows, :]
        kx = k_ref[0, rows, :]
        at32 = -kk * jnp.exp(cum - lw)
        rt32 = r_ref[0, rows, :] * jnp.exp(cum)
        at = at32.astype(BF16)
        rt = rt32.astype(BF16)
        bt = (bb * w_inv).astype(BF16)
        kt = (kx * w_inv).astype(BF16)
        bh = (bb * w_tail).astype(BF16)
        kh = (kx * w_tail).astype(BF16)
        v32 = v_ref[0, rows, :]
        vrot = jnp.concatenate([pltpu.roll(pl_(v32, p), shift=n_, axis=1) for p in range(n_pairs)],
                               axis=1).astype(BF16)

        chains = [(q, p, h) for q in range(nc) for p in range(n_pairs) for h in range(2)]
        aa = []
        for q, p, h in chains:
            lhs = jnp.concatenate([ch_(pl_(at, p), q), ch_(pl_(rt, p), q)], axis=0)
            rhs = jnp.concatenate([ch_(pl_(bt, p), q), ch_(pl_(kt, p), q)], axis=0)
            keep = head0 if h == 0 else jnp.logical_not(head0)
            aa.append(_dot_nt(jnp.where(keep, lhs, jnp.zeros_like(lhs)), rhs))
        a_ab = [jnp.where(tril_strict, a[:c_, :c_], 0.0).astype(BF16) for a in aa]
        a_rb = [jnp.where(tril_incl, a[c_:, :c_], 0.0).astype(BF16) for a in aa]
        a_k = [jnp.concatenate([jnp.where(tril_strict, a[:c_, c_:], 0.0),
                                jnp.where(tril_incl, a[c_:, c_:], 0.0)], axis=0).astype(BF16)
               for a in aa]
        av = [_dot(a_k[i], ch_(pl_(vrot, p), q)) for i, (q, p, h) in enumerate(chains)]
        own = [head0 if h == 0 else jnp.logical_not(head0) for q, p, h in chains]
        xs = [jnp.where(own[i], ch_(pl_(at32, p), q), av[i][:c_])
              for i, (q, p, h) in enumerate(chains)]
        ps = a_ab
        for it in range(6):
            xs = [x + _dot(pw, x.astype(BF16)) for x, pw in zip(xs, ps)]
            if it < 5:
                ps = [_dot(pw, pw).astype(BF16) for pw in ps]
        qos = [jnp.where(own[i], ch_(pl_(rt32, p), q), av[i][c_:]) + _dot(a_rb[i], xs[i].astype(BF16))
               for i, (q, p, h) in enumerate(chains)]
        ops = []
        for q in range(nc):
            for p in range(n_pairs):
                i0 = (q * n_pairs + p) * 2
                g_pair = jnp.where(head0, xs[i0], xs[i0 + 1])
                u_swap = jnp.where(head0, xs[i0 + 1], xs[i0])
                q_pair = jnp.where(head0, qos[i0], qos[i0 + 1])
                o_swap = jnp.where(head0, qos[i0 + 1], qos[i0])
                gu = jnp.concatenate([g_pair, u_swap], axis=1).astype(BF16)
                m1 = _dot_tn(ch_(pl_(bh, p), q), gu)
                m2 = _dot_tn(ch_(pl_(kh, p), q), ch_(pl_(vrot, p), q))
                wc = ch_(pl_(w_chunk, p), q)[0:1, :]
                phi_t = jnp.where(eye_pair, wc, 0.0) + jnp.where(same_head, m1[:, :LANES], 0.0)
                psi_t = jnp.where(same_head, 0.0, m1[:, LANES:] + m2)
                ops.append((q_pair.astype(BF16), o_swap, phi_t.astype(BF16), psi_t))
        for p in range(n_pairs):
            z = z_ref[p]
            outs = []
            for q in range(nc):
                q_pair, o_swap, phi_t, psi_t = ops[q * n_pairs + p]
                zb = z.astype(BF16)
                outs.append(pltpu.roll(_dot(q_pair, zb) + o_swap, shift=n_, axis=1))
                z = _dot(phi_t, zb) + psi_t
            z_ref[p] = z
            o_ref[0, rows, p * LANES:(p + 1) * LANES] = jnp.concatenate(outs, axis=0)
        return carry

    lax.fori_loop(0, n_steps, step_body, 0)


def _wkv_scan(r, lw, k, v, kk, bb, tc=512):
    b_, s_, w_ = r.shape
    tile = lambda b, i: (b, i, 0)
    spec = pl.BlockSpec((1, tc, w_), tile)
    return pl.pallas_call(
        _wkv_kernel,
        grid=(b_, s_ // tc),
        in_specs=[spec] * 6,
        out_specs=spec,
        out_shape=jax.ShapeDtypeStruct((b_, s_, w_), F32),
        scratch_shapes=[pltpu.VMEM((RWKV_HEADS // 2, LANES, LANES), F32)],
        compiler_params=_params(("parallel", "arbitrary")),
        name="wkv_scan",
    )(r, lw, k, v, kk, bb)


def _post_kernel(x_ref, mod_ref, zpc_ref, halo_ref, zg_ref, o_ref, bonus_ref, g_ref,
                 poolw_ref, pscale_ref, pproj_ref, convw_ref, convb_ref, lng_ref, lnb_ref,
                 cproj_ref, lnxg_ref, lnxb_ref, rproj_ref, wo_ref, out_ref, ext_ref):
    i = pl.program_id(1)
    ts = x_ref.shape[1]
    halo = jnp.where(i > 0, halo_ref[0], 0.0)
    cur = zpc_ref[0]

    up = cur[:, :POOL_W]
    ext = jnp.concatenate([halo[:, :POOL_W], up], axis=0)
    e1 = ext + pltpu.roll(ext, shift=1, axis=0)
    e2 = e1 + pltpu.roll(e1, shift=2, axis=0)
    e3 = e2 + pltpu.roll(e2, shift=4, axis=0)
    e4 = e3 + pltpu.roll(e3, shift=8, axis=0)
    lane = lax.broadcasted_iota(jnp.int32, (1, POOL_W), 1)
    grp = lane // POOL_GROUP_W
    pooled = jnp.where(grp == 0, e1, jnp.where(grp == 1, e2, jnp.where(grp == 2, e3, e4)))[HALO:]
    win = jnp.where(grp == 0, 2.0, jnp.where(grp == 1, 4.0, jnp.where(grp == 2, 8.0, 16.0)))
    pos = (i * ts + 1 + lax.broadcasted_iota(jnp.int32, (ts, 1), 0)).astype(F32)
    p = pooled / jnp.minimum(pos, win) - up
    p = _dot(p.astype(BF16), poolw_ref[...]) * pscale_ref[...]
    y_pool = _dot(p.astype(BF16), pproj_ref[...])

    za = jnp.concatenate([halo[:, POOL_W:POOL_W + CONV_W], cur[:, POOL_W:POOL_W + CONV_W]], axis=0)
    zb = jnp.concatenate([halo[:, POOL_W + CONV_W:], cur[:, POOL_W + CONV_W:]], axis=0)
    u = za * _sigmoid(zb)
    ext_ref[0] = u
    for r in range(1, SUBLANES):
        ext_ref[r] = pltpu.roll(u, shift=u.shape[0] - r, axis=0)
    acc = jnp.zeros((ts, CONV_W), F32) + convb_ref[...]
    for j in range(CONV_K):
        st = HALO - (CONV_K - 1) + j
        al = st - st % SUBLANES
        acc = acc + ext_ref[st % SUBLANES, al:al + ts, :] * convw_ref[j:j + 1, :]
    mean = jnp.mean(acc, axis=-1, keepdims=True)
    dev = acc - mean
    var = jnp.mean(dev * dev, axis=-1, keepdims=True)
    un = dev * lax.rsqrt(var + LN_EPS) * lng_ref[...] + lnb_ref[...]
    y_conv = _dot(_silu(un).astype(BF16), cproj_ref[...])

    o = o_ref[0]
    inv_n = 1.0 / RWKV_HEAD
    mu = _head_sum(o) * inv_n
    od = o - mu
    ovar = _head_sum(od * od) * inv_n
    on = od * lax.rsqrt(ovar + GN_EPS) * lnxg_ref[...] + lnxb_ref[...]
    y_rwkv = _dot(((on + bonus_ref[0]) * g_ref[0]).astype(BF16), rproj_ref[...])

    d = x_ref.shape[2]
    zg = zg_ref[0]
    merged = (zg[:, :d].astype(F32) * y_pool + zg[:, d:2 * d].astype(F32) * y_conv
              + zg[:, 2 * d:].astype(F32) * y_rwkv)
    out_ref[0] = x_ref[0] + mod_ref[0, 2:3, :] * _dot(merged.astype(BF16), wo_ref[...])


def _post(x, mod, zpc, zg, o, bonus, g, poolw, pscale, pproj, convw, convb, lng, lnb, cproj,
          lnxg, lnxb, rproj, wo, ts=256):
    b_, s_, d = x.shape
    const = lambda b, i: (0, 0)
    tile = lambda b, i: (b, i, 0)
    full = lambda a: pl.BlockSpec(a.shape, const)
    in_specs = [pl.BlockSpec((1, ts, d), tile),
                pl.BlockSpec((1, 3, d), lambda b, i: (b, 0, 0)),
                pl.BlockSpec((1, ts, zpc.shape[2]), tile),
                pl.BlockSpec((1, HALO, zpc.shape[2]),
                             lambda b, i: (b, jnp.maximum(i * (ts // HALO) - 1, 0), 0)),
                pl.BlockSpec((1, ts, zg.shape[2]), tile),
                pl.BlockSpec((1, ts, RWKV_W), tile),
                pl.BlockSpec((1, ts, RWKV_W), tile),
                pl.BlockSpec((1, ts, RWKV_W), tile)]
    weights = [poolw, pscale, pproj, convw, convb, lng, lnb, cproj, lnxg, lnxb, rproj, wo]
    in_specs += [full(a) for a in weights]
    return pl.pallas_call(
        _post_kernel,
        grid=(b_, s_ // ts),
        in_specs=in_specs,
        out_specs=pl.BlockSpec((1, ts, d), tile),
        out_shape=jax.ShapeDtypeStruct((b_, s_, d), F32),
        scratch_shapes=[pltpu.VMEM((SUBLANES, HALO + ts, CONV_W), F32)],
        compiler_params=_params(("parallel", "parallel")),
        name="mix_merge",
    )(x, mod, zpc, zpc, zg, o, bonus, g, *weights)


def _router_kernel(x_ref, mod_ref, g_ref, wr_ref, gate_ref, h_ref):
    h = _ada_rms(x_ref[0], mod_ref[0], g_ref[...])
    h_ref[0] = h.astype(BF16)
    logits = jnp.dot(h, wr_ref[...], preferred_element_type=F32, precision=lax.Precision.HIGHEST)
    n_e = logits.shape[1]
    lane = lax.broadcasted_iota(jnp.int32, logits.shape, 1)
    m1 = jnp.max(logits, axis=-1, keepdims=True)
    i1 = jnp.min(jnp.where(logits == m1, lane, n_e), axis=-1, keepdims=True)
    sel1 = lane == i1
    rest = jnp.where(sel1, -jnp.inf, logits)
    m2 = jnp.max(rest, axis=-1, keepdims=True)
    i2 = jnp.min(jnp.where(rest == m2, lane, n_e), axis=-1, keepdims=True)
    sel2 = lane == i2
    e2 = jnp.exp(m2 - m1)
    w1 = 1.0 / (1.0 + e2)
    gate_ref[0] = jnp.where(sel1, w1, 0.0) + jnp.where(sel2, e2 * w1, 0.0)


def _router(x, mod, g, wr, ts=512):
    b_, s_, d = x.shape
    n_e = wr.shape[1]
    tile = lambda b, i: (b, i, 0)
    return pl.pallas_call(
        _router_kernel,
        grid=(b_, s_ // ts),
        in_specs=[pl.BlockSpec((1, ts, d), tile),
                  pl.BlockSpec((1, 3, d), lambda b, i: (b, 0, 0)),
                  pl.BlockSpec((1, d), lambda b, i: (0, 0)),
                  pl.BlockSpec(wr.shape, lambda b, i: (0, 0))],
        out_specs=[pl.BlockSpec((1, ts, n_e), tile), pl.BlockSpec((1, ts, d), tile)],
        out_shape=[jax.ShapeDtypeStruct((b_, s_, n_e), F32), jax.ShapeDtypeStruct((b_, s_, d), BF16)],
        compiler_params=_params(("parallel", "parallel")),
        name="moe_router",
    )(x, mod, g, wr)


PIECE = 128
SUB = 256
SPAN = SUB + PIECE


def _moe_kernel(cnt_ref, first_ref, h_ref, rank_em_ref, rank_tm_ref, gate_ref,
                w1_ref, w3_ref, w2_ref, y_ref, xs_ref, acc_ref):
    b = pl.program_id(0)
    e = pl.program_id(1)
    f = pl.program_id(2)
    n_e = pl.num_programs(1)
    tb = h_ref.shape[0]
    n_sub = tb // SUB
    pair = b * n_e + e
    n_pieces = (cnt_ref[pair] + (PIECE - 1)) // PIECE
    n_big = n_pieces // 2
    has_tail = (n_pieces % 2) == 1

    def span_rows(s):
        return pl.ds(pl.multiple_of(first_ref[pair * n_sub + s] * PIECE, PIECE), SPAN)

    def for_groups(fn):
        def big(i, carry):
            fn(2 * i, 2 * PIECE)
            return carry
        lax.fori_loop(0, n_big, big, 0)

        @pl.when(has_tail)
        def _():
            fn(2 * n_big, PIECE)

    def rows_of(j0, m):
        return pl.ds(pl.multiple_of(j0 * PIECE, PIECE), m)

    @pl.when((e == 0) & (f == 0))
    def _():
        y_ref[...] = jnp.zeros_like(y_ref)

    @pl.when(f == 0)
    def _():
        def clear(j, carry):
            xs_ref[rows_of(j, PIECE), :] = jnp.zeros((PIECE, xs_ref.shape[1]), BF16)
            return carry
        lax.fori_loop(0, jnp.minimum(n_pieces + SPAN // PIECE, tb // PIECE), clear, 0)
        acc_ref[pl.ds(pl.multiple_of(jnp.minimum(n_pieces * PIECE, tb - SPAN), PIECE), SPAN), :] = (
            jnp.zeros((SPAN, acc_ref.shape[1]), F32))
        for s in range(n_sub):
            rows = span_rows(s)
            slot = lax.broadcasted_iota(jnp.int32, (SPAN, 1), 0) + first_ref[pair * n_sub + s] * PIECE
            onehot = jnp.where(rank_em_ref[0, 0, s] == slot, 1.0, 0.0).astype(BF16)
            xs_ref[rows, :] += _dot(onehot, h_ref[s * SUB:(s + 1) * SUB, :]).astype(BF16)

    def expert(j0, m):
        rows = rows_of(j0, m)
        xj = xs_ref[rows, :]
        act = _silu(_dot(xj, w1_ref[0])) * _dot(xj, w3_ref[0])
        part = _dot(act.astype(BF16), w2_ref[0])

        @pl.when(f == 0)
        def _():
            acc_ref[rows, :] = part

        @pl.when(f > 0)
        def _():
            acc_ref[rows, :] += part
    for_groups(expert)

    @pl.when(f == pl.num_programs(2) - 1)
    def _():
        lane_e = lax.broadcasted_iota(jnp.int32, (1, n_e), 1) == e
        for s in range(n_sub):
            tok = slice(s * SUB, (s + 1) * SUB)
            slot = lax.broadcasted_iota(jnp.int32, (1, SPAN), 1) + first_ref[pair * n_sub + s] * PIECE
            rank_col = jnp.sum(jnp.where(lane_e, rank_tm_ref[tok, :], 0), axis=-1, keepdims=True)
            gate_col = jnp.sum(jnp.where(lane_e, gate_ref[tok, :], 0.0), axis=-1, keepdims=True)
            onehot_t = jnp.where(rank_col == slot, 1.0, 0.0).astype(BF16)
            y_ref[tok, :] += gate_col * _dot(onehot_t, acc_ref[span_rows(s), :].astype(BF16))


def _moe(h, gate, w1, w3, w2, tb=2048, tf=1792):
    t_, d = h.shape
    n_e, _, ff = w1.shape
    nb = t_ // tb
    n_sub = tb // SUB
    sel = (gate != 0.0).reshape(nb, tb, n_e).astype(jnp.int32)
    rank_tm = jnp.where(sel > 0, jnp.cumsum(sel, axis=1) - 1, -1)
    ends = jnp.cumsum(jnp.sum(sel.reshape(nb, n_sub, SUB, n_e), axis=2), axis=1)
    ends = jnp.transpose(ends, (0, 2, 1))
    cnt = ends[:, :, -1]
    starts = jnp.concatenate([jnp.zeros_like(ends[:, :, :1]), ends[:, :, :-1]], axis=2)
    first = jnp.minimum(starts // PIECE, (tb - SPAN) // PIECE)
    rank_em = jnp.transpose(rank_tm, (0, 2, 1)).reshape(nb, n_e, n_sub, 1, SUB)
    once = pl.Buffered(1)
    grid_spec = pltpu.PrefetchScalarGridSpec(
        num_scalar_prefetch=2,
        grid=(nb, n_e, ff // tf),
        in_specs=[pl.BlockSpec((tb, d), lambda b, e, f, *_: (b, 0), pipeline_mode=once),
                  pl.BlockSpec((1, 1, n_sub, 1, SUB), lambda b, e, f, *_: (b, e, 0, 0, 0)),
                  pl.BlockSpec((tb, n_e), lambda b, e, f, *_: (b, 0), pipeline_mode=once),
                  pl.BlockSpec((tb, n_e), lambda b, e, f, *_: (b, 0), pipeline_mode=once),
                  pl.BlockSpec((1, d, tf), lambda b, e, f, *_: (e, 0, f)),
                  pl.BlockSpec((1, d, tf), lambda b, e, f, *_: (e, 0, f)),
                  pl.BlockSpec((1, tf, d), lambda b, e, f, *_: (e, f, 0))],
        out_specs=pl.BlockSpec((tb, d), lambda b, e, f, *_: (b, 0), pipeline_mode=once),
        scratch_shapes=[pltpu.VMEM((tb, d), BF16), pltpu.VMEM((tb, d), F32)])
    return pl.pallas_call(
        _moe_kernel,
        grid_spec=grid_spec,
        out_shape=jax.ShapeDtypeStruct((t_, d), F32),
        compiler_params=_params(("parallel", "arbitrary", "arbitrary")),
        name="moe_sparse",
    )(cnt.reshape(-1), first.reshape(-1), h, rank_em, rank_tm.reshape(t_, n_e), gate, w1, w3, w2)


def _ffn_kernel(x_ref, mod_ref, g_ref, w1_ref, w3_ref, w2_ref, out_ref):
    x = x_ref[0]
    h = _ada_rms(x, mod_ref[0], g_ref[...]).astype(BF16)
    act = _silu(_dot(h, w1_ref[...])) * _dot(h, w3_ref[...])
    out_ref[0] = x + mod_ref[0, 2:3, :] * _dot(act.astype(BF16), w2_ref[...])


def _ffn(x, mod, g, w1, w3, w2, tm=256):
    b_, s_, d = x.shape
    tile = lambda b, i: (b, i, 0)
    const = lambda b, i: (0, 0)
    once = pl.Buffered(1)
    return pl.pallas_call(
        _ffn_kernel,
        grid=(b_, s_ // tm),
        in_specs=[pl.BlockSpec((1, tm, d), tile),
                  pl.BlockSpec((1, 3, d), lambda b, i: (b, 0, 0)),
                  pl.BlockSpec((1, d), const),
                  pl.BlockSpec(w1.shape, const, pipeline_mode=once),
                  pl.BlockSpec(w3.shape, const, pipeline_mode=once),
                  pl.BlockSpec(w2.shape, const, pipeline_mode=once)],
        out_specs=pl.BlockSpec((1, tm, d), tile),
        out_shape=jax.ShapeDtypeStruct((b_, s_, d), F32),
        compiler_params=_params(("parallel", "parallel")),
        name="ffn",
    )(x, mod, g, w1, w3, w2)


def _finish_kernel(has_y, has_norm, x_ref, *rest):
    x = x_ref[0]
    if has_y:
        y_ref, mod_ref = rest[:2]
        rest = rest[2:]
        x = x + mod_ref[0, 2:3, :] * y_ref[0]
    if has_norm:
        g_ref = rest[0]
        rest = rest[1:]
        x = x * lax.rsqrt(jnp.mean(x * x, axis=-1, keepdims=True) + RMS_EPS) * g_ref[...]
    rest[0][0] = x


def _finish(x, y=None, mod=None, g=None, ts=512):
    b_, s_, d = x.shape
    tile = lambda b, i: (b, i, 0)
    in_specs = [pl.BlockSpec((1, ts, d), tile)]
    args = [x]
    if y is not None:
        in_specs += [pl.BlockSpec((1, ts, d), tile), pl.BlockSpec((1, 3, d), lambda b, i: (b, 0, 0))]
        args += [y, mod]
    if g is not None:
        in_specs.append(pl.BlockSpec((1, d), lambda b, i: (0, 0)))
        args.append(g)
    return pl.pallas_call(
        functools.partial(_finish_kernel, y is not None, g is not None),
        grid=(b_, s_ // ts),
        in_specs=in_specs,
        out_specs=pl.BlockSpec((1, ts, d), tile),
        out_shape=jax.ShapeDtypeStruct((b_, s_, d), F32),
        compiler_params=_params(("parallel", "parallel")),
        name="finish",
    )(*args)


def _block_diag(w):
    g_, c_, _ = w.shape
    out = jnp.zeros((g_ * c_, g_ * c_), w.dtype)
    for gi in range(g_):
        out = out.at[gi * c_:(gi + 1) * c_, gi * c_:(gi + 1) * c_].set(w[gi])
    return out


def kernel(x, c, ada_w, ada_b, norm_g, w_in, pool_w, pool_scale, pool_proj, conv_w, conv_b, conv_ln_g, conv_ln_b, conv_proj, rwkv_mu, rwkv_w0, rwkv_w2, rwkv_a0, rwkv_a2, rwkv_g2, rwkv_kk_scale, rwkv_ka, rwkv_rk, rwkv_lnx_g, rwkv_lnx_b, rwkv_proj, vres_w_down, vres_mu, vres_v0, vres_v2, w_o, ffn_w1, ffn_w3, ffn_w2, moe_router, moe_w1, moe_w3, moe_w2, final_norm_g):
    depth = w_in.shape[0]
    d = x.shape[2]
    row = lambda a: a.reshape(1, -1)
    mods = _ada_mods(c, ada_w, ada_b)
    v_first = None
    for l in range(depth):
        mod = mods[2 * l]
        zr_pad = jnp.zeros((d, ZR_COLS - RWKV_COLS - V_LORA), F32)
        mu_pad = jnp.zeros((ZR_COLS - RWKV_COLS - V_LORA,), F32)
        if l == 0:
            w_vd = jnp.zeros((d, V_LORA), F32)
            mu_vd = jnp.zeros((V_LORA,), F32)
        else:
            w_vd = vres_w_down[l - 1]
            mu_vd = vres_mu[l - 1]
        wr = jnp.concatenate([w_in[l][:, OFF_RWKV:OFF_GATE], w_vd, zr_pad], axis=1).astype(BF16)
        mu = jnp.concatenate([rwkv_mu[l], mu_vd, mu_pad]).reshape(1, ZR_COLS)
        wpc = w_in[l][:, :OFF_RWKV].astype(BF16)
        wg = w_in[l][:, OFF_GATE:OFF_GATE + 3 * d].astype(BF16)
        vres = None
        if l > 0:
            vres = (v_first, row(vres_v0[l - 1]), vres_v2[l - 1].astype(BF16))
        zpc, zg, r, lw, kh, v, kk, bb, g, bonus = _in_proj(
            x, mod, row(norm_g[l, 0]), wpc, wr, wg,
            mu, row(rwkv_w0[l]), rwkv_w2[l].astype(BF16), row(rwkv_a0[l]),
            rwkv_a2[l].astype(BF16), rwkv_g2[l].astype(BF16), row(rwkv_kk_scale[l]),
            row(rwkv_ka[l]), row(rwkv_rk[l]), vres)
        if l == 0:
            v_first = v
        o = _wkv_scan(r, lw, kh, v, kk, bb)
        x = _post(x, mod, zpc, zg, o, bonus, g,
                  _block_diag(pool_w[l]).astype(BF16), row(pool_scale[l]), pool_proj[l].astype(BF16),
                  conv_w[l], row(conv_b[l]), row(conv_ln_g[l]), row(conv_ln_b[l]),
                  conv_proj[l].astype(BF16), row(rwkv_lnx_g[l]), row(rwkv_lnx_b[l]),
                  rwkv_proj[l].astype(BF16), w_o[l].astype(BF16))
        mod = mods[2 * l + 1]
        gn = row(norm_g[l, 1])
        i = l // 2
        last = l == depth - 1
        if l % 2 == 0:
            x = _ffn(x, mod, gn, ffn_w1[i].astype(BF16), ffn_w3[i].astype(BF16), ffn_w2[i].astype(BF16))
            if last:
                x = _finish(x, g=row(final_norm_g))
        else:
            gate, h = _router(x, mod, gn, moe_router[i])
            n_e = gate.shape[2]
            y = _moe(h.reshape(-1, d), gate.reshape(-1, n_e), moe_w1[i].astype(BF16),
                     moe_w3[i].astype(BF16), moe_w2[i].astype(BF16)).reshape(x.shape)
            x = _finish(x, y, mod, row(final_norm_g) if last else None)
    return x
```

```python
import functools

import jax
import jax.numpy as jnp
from jax import lax
from jax.experimental import pallas as pl
from jax.experimental.pallas import tpu as pltpu

F32 = jnp.float32
BF16 = jnp.bfloat16

POOL_WINDOWS = (2, 4, 8, 16)
POOL_GROUP_W = 64
POOL_W = 256
CONV_W = 256
CONV_K = 31
RWKV_HEAD = 64
RWKV_W = 512
RWKV_HEADS = 8
W_LORA = 32
A_LORA = 32
V_LORA = 16
G_LORA = 64
RWKV_COLS = 3 * RWKV_W + W_LORA + A_LORA + G_LORA
ZR_COLS = 1792
OFF_CONV = POOL_W
OFF_RWKV = OFF_CONV + 2 * CONV_W
OFF_GATE = OFF_RWKV + RWKV_COLS
N_EXPERTS = 8
RMS_EPS = 1e-6
LN_EPS = 1e-5
GN_EPS = 64e-5

LANES = 128
SUBLANES = 8
VMEM_LIMIT = 56 * 1024 * 1024
CHUNK = 64
HALO = 32


def _sigmoid(x):
    return 1.0 / (1.0 + jnp.exp(-x))


def _silu(x):
    return x * _sigmoid(x)


def _dot(a, b):
    return jnp.dot(a, b, preferred_element_type=F32)


def _ada_rms(x, mod, g):
    y = x * lax.rsqrt(jnp.mean(x * x, axis=-1, keepdims=True) + RMS_EPS) * g
    return y * (1.0 + mod[1:2, :]) + mod[0:1, :]


def _params(sem):
    return pltpu.CompilerParams(dimension_semantics=sem, vmem_limit_bytes=VMEM_LIMIT)


def _mod_kernel(c_ref, w_ref, b_ref, o_ref):
    c = c_ref[...]
    o_ref[0] = jnp.dot(_silu(c), w_ref[0], preferred_element_type=F32,
                       precision=lax.Precision.HIGHEST) + b_ref[0]


def _ada_mods(c, ada_w, ada_b):
    b_, d = c.shape
    n_sub = ada_w.shape[0] * ada_w.shape[1]
    w = ada_w.reshape(n_sub, d, 3 * d)
    bias = ada_b.reshape(n_sub, 1, 3 * d)
    c8 = jnp.zeros((SUBLANES, d), F32).at[:b_].set(c)
    tn = 1024
    out = pl.pallas_call(
        _mod_kernel,
        grid=(n_sub, 3 * d // tn),
        in_specs=[pl.BlockSpec((SUBLANES, d), lambda s, j: (0, 0)),
                  pl.BlockSpec((1, d, tn), lambda s, j: (s, 0, j)),
                  pl.BlockSpec((1, 1, tn), lambda s, j: (s, 0, j))],
        out_specs=pl.BlockSpec((1, SUBLANES, tn), lambda s, j: (s, 0, j)),
        out_shape=jax.ShapeDtypeStruct((n_sub, SUBLANES, 3 * d), F32),
        compiler_params=_params(("parallel", "parallel")),
        name="ada_mod",
    )(c8, w, bias)
    return out[:, :b_].reshape(n_sub, b_, 3, d)


def _head_sum(x):
    low = lax.broadcasted_iota(jnp.int32, (1, LANES), 1) < RWKV_HEAD
    outs = []
    for cb in range(x.shape[1] // LANES):
        xc = x[:, cb * LANES:(cb + 1) * LANES]
        s_lo = jnp.sum(jnp.where(low, xc, 0.0), axis=-1, keepdims=True)
        s_hi = jnp.sum(jnp.where(low, 0.0, xc), axis=-1, keepdims=True)
        outs.append(jnp.where(low, s_lo, s_hi))
    return jnp.concatenate(outs, axis=-1)


def _inproj_kernel(has_vres, x_ref, mod_ref, gn_ref, wpc_ref, wr_ref, wg_ref, mu_ref, w0_ref, w2_ref,
                   a0_ref, a2_ref, g2_ref, ksc_ref, ka_ref, rk_ref, *rest):
    if has_vres:
        vf_ref, v0_ref, v2_ref = rest[:3]
        rest = rest[3:]
    zpc_ref, zg_ref, r_ref, lw_ref, k_ref, v_ref, kk_ref, b_ref, g_ref, bonus_ref, last_ref = rest

    i = pl.program_id(1)
    hb = _ada_rms(x_ref[0], mod_ref[0], gn_ref[...]).astype(BF16)
    z = _dot(hb, wr_ref[...])
    ts = z.shape[0]
    prev = jnp.where(i > 0, last_ref[SUBLANES - 1:SUBLANES, :], 0.0)
    last_ref[...] = z[ts - SUBLANES:, :]
    row = lax.broadcasted_iota(jnp.int32, (ts, 1), 0)
    zsh = jnp.where(row == 0, prev, pltpu.roll(z, shift=1, axis=0))
    z = z + (zsh - z) * mu_ref[...]

    w3 = RWKV_W
    r = z[:, 0:w3]
    k = z[:, w3:2 * w3]
    v = z[:, 2 * w3:3 * w3]
    o = 3 * w3
    wd = z[:, o:o + W_LORA]
    ad = z[:, o + W_LORA:o + W_LORA + A_LORA]
    gd = z[:, o + W_LORA + A_LORA:o + W_LORA + A_LORA + G_LORA]

    wpre = w0_ref[...] + _dot(jnp.tanh(wd).astype(BF16), w2_ref[...])
    y = -wpre
    softplus = jnp.maximum(y, 0.0) + jnp.log(1.0 + jnp.exp(-jnp.abs(y)))
    w = -softplus - 0.5
    lw_ref[0] = -jnp.exp(w)
    a = _sigmoid(a0_ref[...] + _dot(ad.astype(BF16), a2_ref[...]))
    g_ref[0] = _dot(_sigmoid(gd).astype(BF16), g2_ref[...])
    if has_vres:
        vd = z[:, RWKV_COLS:RWKV_COLS + V_LORA]
        mix = _sigmoid(v0_ref[...] + _dot(vd.astype(BF16), v2_ref[...]))
        v = v + (vf_ref[0] - v) * mix
    kk = k * ksc_ref[...]
    kk = kk / jnp.maximum(jnp.sqrt(_head_sum(kk * kk)), 1e-12)
    kh = k * (1.0 + (a - 1.0) * ka_ref[...])
    r_ref[0] = r
    k_ref[0] = kh
    v_ref[0] = v
    kk_ref[0] = kk
    b_ref[0] = kk * a
    bonus_ref[0] = _head_sum(r * kh * rk_ref[...]) * v
    zpc_ref[0] = _dot(hb, wpc_ref[...])
    zg_ref[0] = _sigmoid(_dot(hb, wg_ref[...])).astype(BF16)


def _in_proj(x, mod, gn, wpc, wr, wg, mu, w0, w2, a0, a2, g2, ksc, ka, rk, vres, ts=512):
    b_, s_, d = x.shape
    zc = wr.shape[1]
    has_vres = vres is not None
    const = lambda b, i: (0, 0)
    tile = lambda b, i: (b, i, 0)
    vec = pl.BlockSpec((1, RWKV_W), const)
    in_specs = [pl.BlockSpec((1, ts, d), tile),
                pl.BlockSpec((1, 3, d), lambda b, i: (b, 0, 0)),
                pl.BlockSpec((1, d), const),
                pl.BlockSpec(wpc.shape, const, pipeline_mode=pl.Buffered(1)),
                pl.BlockSpec(wr.shape, const, pipeline_mode=pl.Buffered(1)),
                pl.BlockSpec(wg.shape, const, pipeline_mode=pl.Buffered(1)),
                pl.BlockSpec((1, zc), const),
                vec, pl.BlockSpec((W_LORA, RWKV_W), const),
                vec, pl.BlockSpec((A_LORA, RWKV_W), const),
                pl.BlockSpec((G_LORA, RWKV_W), const),
                vec, vec, vec]
    args = [x, mod, gn, wpc, wr, wg, mu, w0, w2, a0, a2, g2, ksc, ka, rk]
    if has_vres:
        v_first, v0, v2 = vres
        in_specs += [pl.BlockSpec((1, ts, RWKV_W), tile), vec,
                     pl.BlockSpec((V_LORA, RWKV_W), const)]
        args += [v_first, v0, v2]
    out = jax.ShapeDtypeStruct((b_, s_, RWKV_W), F32)
    return pl.pallas_call(
        functools.partial(_inproj_kernel, has_vres),
        grid=(b_, s_ // ts),
        in_specs=in_specs,
        out_specs=[pl.BlockSpec((1, ts, wpc.shape[1]), tile),
                   pl.BlockSpec((1, ts, wg.shape[1]), tile)] + [pl.BlockSpec((1, ts, RWKV_W), tile)] * 8,
        out_shape=[jax.ShapeDtypeStruct((b_, s_, wpc.shape[1]), F32),
                   jax.ShapeDtypeStruct((b_, s_, wg.shape[1]), BF16)] + [out] * 8,
        scratch_shapes=[pltpu.VMEM((SUBLANES, zc), F32)],
        compiler_params=_params(("parallel", "arbitrary")),
        name="in_proj",
    )(*args)


def _split3(x):
    hi = x.astype(BF16)
    r1 = x - hi.astype(F32)
    mid = r1.astype(BF16)
    lo = (r1 - mid.astype(F32)).astype(BF16)
    return hi, mid, lo


def _dot_nt(a, b):
    return lax.dot_general(a, b, (((1,), (1,)), ((), ())), preferred_element_type=F32)


def _dot_tn(a, b):
    return lax.dot_general(a, b, (((0,), (0,)), ((), ())), preferred_element_type=F32)


CHUNKS_PER_STEP = 4


def _wkv_kernel(r_ref, lw_ref, k_ref, v_ref, kk_ref, b_ref, o_ref, z_ref):
    @pl.when(pl.program_id(1) == 0)
    def _():
        z_ref[...] = jnp.zeros_like(z_ref)

    c_ = CHUNK
    n_ = RWKV_HEAD
    nc = CHUNKS_PER_STEP
    n_pairs = RWKV_HEADS // 2
    n_steps = r_ref.shape[1] // (c_ * nc)
    ri = lax.broadcasted_iota(jnp.int32, (c_, c_), 0)
    ci = lax.broadcasted_iota(jnp.int32, (c_, c_), 1)
    tril_incl = (ri >= ci)
    tril_strict = (ri > ci)
    rs = lax.broadcasted_iota(jnp.int32, (nc * c_, nc * c_), 0)
    cs = lax.broadcasted_iota(jnp.int32, (nc * c_, nc * c_), 1)
    tri_b = jnp.where((rs >= cs) & ((rs // c_) == (cs // c_)), 1.0, 0.0).astype(BF16)
    rl = lax.broadcasted_iota(jnp.int32, (LANES, LANES), 0)
    cl = lax.broadcasted_iota(jnp.int32, (LANES, LANES), 1)
    eye_pair = (rl == cl)
    same_head = ((rl < n_) == (cl < n_))
    head0 = lax.broadcasted_iota(jnp.int32, (1, LANES), 1) < n_
    pl_ = lambda a, p: a[:, p * LANES:(p + 1) * LANES]
    ch_ = lambda a, q: a[q * c_:(q + 1) * c_]

    def step_body(st, carry):
        rows = pl.ds(pl.multiple_of(st * (nc * c_), nc * c_), nc * c_)
        lw = lw_ref[0, rows, :]
        hi, mid, lo = _split3(lw)
        cum = _dot(tri_b, hi) + _dot(tri_b, mid) + _dot(tri_b, lo)
        cum_last = jnp.concatenate(
            [jnp.broadcast_to(cum[(q + 1) * c_ - 1:(q + 1) * c_, :], (c_, cum.shape[1])) for q in range(nc)],
            axis=0)
        w_inv = jnp.exp(-cum)
        w_tail = jnp.exp(cum_last - cum)
        w_chunk = jnp.exp(cum_last)
        kk = kk_ref[0, rows, :]
        bb = b_ref[0, rows, :]
        kx = k_ref[0, rows, :]
        at32 = -kk * jnp.exp(cum - lw)
        rt32 = r_ref[0, rows, :] * jnp.exp(cum)
        at = at32.astype(BF16)
        rt = rt32.astype(BF16)
        bt = (bb * w_inv).astype(BF16)
        kt = (kx * w_inv).astype(BF16)
        bh = (bb * w_tail).astype(BF16)
        kh = (kx * w_tail).astype(BF16)
        v32 = v_ref[0, rows, :]
        vrot = jnp.concatenate([pltpu.roll(pl_(v32, p), shift=n_, axis=1) for p in range(n_pairs)],
                               axis=1).astype(BF16)

        chains = [(q, p, h) for q in range(nc) for p in range(n_pairs) for h in range(2)]
        aa = []
        for q, p, h in chains:
            lhs = jnp.concatenate([ch_(pl_(at, p), q), ch_(pl_(rt, p), q)], axis=0)
            rhs = jnp.concatenate([ch_(pl_(bt, p), q), ch_(pl_(kt, p), q)], axis=0)
            keep = head0 if h == 0 else jnp.logical_not(head0)
            aa.append(_dot_nt(jnp.where(keep, lhs, jnp.zeros_like(lhs)), rhs))
        a_ab = [jnp.where(tril_strict, a[:c_, :c_], 0.0).astype(BF16) for a in aa]
        a_rb = [jnp.where(tril_incl, a[c_:, :c_], 0.0).astype(BF16) for a in aa]
        a_k = [jnp.concatenate([jnp.where(tril_strict, a[:c_, c_:], 0.0),
                                jnp.where(tril_incl, a[c_:, c_:], 0.0)], axis=0).astype(BF16)
               for a in aa]
        av = [_dot(a_k[i], ch_(pl_(vrot, p), q)) for i, (q, p, h) in enumerate(chains)]
        own = [head0 if h == 0 else jnp.logical_not(head0) for q, p, h in chains]
        xs = [jnp.where(own[i], ch_(pl_(at32, p), q), av[i][:c_])
              for i, (q, p, h) in enumerate(chains)]
        ps = a_ab
        for it in range(6):
            xs = [x + _dot(pw, x.astype(BF16)) for x, pw in zip(xs, ps)]
            if it < 5:
                ps = [_dot(pw, pw).astype(BF16) for pw in ps]
        qos = [jnp.where(own[i], ch_(pl_(rt32, p), q), av[i][c_:]) + _dot(a_rb[i], xs[i].astype(BF16))
               for i, (q, p, h) in enumerate(chains)]
        ops = []
        for q in range(nc):
            for p in range(n_pairs):
                i0 = (q * n_pairs + p) * 2
                g_pair = jnp.where(head0, xs[i0], xs[i0 + 1])
                u_swap = jnp.where(head0, xs[i0 + 1], xs[i0])
                q_pair = jnp.where(head0, qos[i0], qos[i0 + 1])
                o_swap = jnp.where(head0, qos[i0 + 1], qos[i0])
                gu = jnp.concatenate([g_pair, u_swap], axis=1).astype(BF16)
                m1 = _dot_tn(ch_(pl_(bh, p), q), gu)
                m2 = _dot_tn(ch_(pl_(kh, p), q), ch_(pl_(vrot, p), q))
                wc = ch_(pl_(w_chunk, p), q)[0:1, :]
                phi_t = jnp.where(eye_pair, wc, 0.0) + jnp.where(same_head, m1[:, :LANES], 0.0)
                psi_t = jnp.where(same_head, 0.0, m1[:, LANES:] + m2)
                ops.append((q_pair.astype(BF16), o_swap, phi_t.astype(BF16), psi_t))
        for p in range(n_pairs):
            z = z_ref[p]
            outs = []
            for q in range(nc):
                q_pair, o_swap, phi_t, psi_t = ops[q * n_pairs + p]
                zb = z.astype(BF16)
                outs.append(pltpu.roll(_dot(q_pair, zb) + o_swap, shift=n_, axis=1))
                z = _dot(phi_t, zb) + psi_t
            z_ref[p] = z
            o_ref[0, rows, p * LANES:(p + 1) * LANES] = jnp.concatenate(outs, axis=0)
        return carry

    lax.fori_loop(0, n_steps, step_body, 0)


def _wkv_scan(r, lw, k, v, kk, bb, tc=1024):
    b_, s_, w_ = r.shape
    tile = lambda b, i: (b, i, 0)
    spec = pl.BlockSpec((1, tc, w_), tile)
    return pl.pallas_call(
        _wkv_kernel,
        grid=(b_, s_ // tc),
        in_specs=[spec] * 6,
        out_specs=spec,
        out_shape=jax.ShapeDtypeStruct((b_, s_, w_), F32),
        scratch_shapes=[pltpu.VMEM((RWKV_HEADS // 2, LANES, LANES), F32)],
        compiler_params=_params(("parallel", "arbitrary")),
        name="wkv_scan",
    )(r, lw, k, v, kk, bb)


def _post_kernel(x_ref, mod_ref, zpc_ref, halo_ref, zg_ref, o_ref, bonus_ref, g_ref,
                 poolw_ref, pscale_ref, pproj_ref, convw_ref, convb_ref, lng_ref, lnb_ref,
                 cproj_ref, lnxg_ref, lnxb_ref, rproj_ref, wo_ref, out_ref, ext_ref):
    i = pl.program_id(1)
    ts = x_ref.shape[1]
    halo = jnp.where(i > 0, halo_ref[0], 0.0)
    cur = zpc_ref[0]

    up = cur[:, :POOL_W]
    ext = jnp.concatenate([halo[:, :POOL_W], up], axis=0)
    e1 = ext + pltpu.roll(ext, shift=1, axis=0)
    e2 = e1 + pltpu.roll(e1, shift=2, axis=0)
    e3 = e2 + pltpu.roll(e2, shift=4, axis=0)
    e4 = e3 + pltpu.roll(e3, shift=8, axis=0)
    lane = lax.broadcasted_iota(jnp.int32, (1, POOL_W), 1)
    grp = lane // POOL_GROUP_W
    pooled = jnp.where(grp == 0, e1, jnp.where(grp == 1, e2, jnp.where(grp == 2, e3, e4)))[HALO:]
    win = jnp.where(grp == 0, 2.0, jnp.where(grp == 1, 4.0, jnp.where(grp == 2, 8.0, 16.0)))
    pos = (i * ts + 1 + lax.broadcasted_iota(jnp.int32, (ts, 1), 0)).astype(F32)
    p = pooled / jnp.minimum(pos, win) - up
    p = _dot(p.astype(BF16), poolw_ref[...]) * pscale_ref[...]
    y_pool = _dot(p.astype(BF16), pproj_ref[...])

    za = jnp.concatenate([halo[:, POOL_W:POOL_W + CONV_W], cur[:, POOL_W:POOL_W + CONV_W]], axis=0)
    zb = jnp.concatenate([halo[:, POOL_W + CONV_W:], cur[:, POOL_W + CONV_W:]], axis=0)
    u = za * _sigmoid(zb)
    ext_ref[0] = u
    for r in range(1, SUBLANES):
        ext_ref[r] = pltpu.roll(u, shift=u.shape[0] - r, axis=0)
    acc = jnp.zeros((ts, CONV_W), F32) + convb_ref[...]
    for j in range(CONV_K):
        st = HALO - (CONV_K - 1) + j
        al = st - st % SUBLANES
        acc = acc + ext_ref[st % SUBLANES, al:al + ts, :] * convw_ref[j:j + 1, :]
    mean = jnp.mean(acc, axis=-1, keepdims=True)
    dev = acc - mean
    var = jnp.mean(dev * dev, axis=-1, keepdims=True)
    un = dev * lax.rsqrt(var + LN_EPS) * lng_ref[...] + lnb_ref[...]
    y_conv = _dot(_silu(un).astype(BF16), cproj_ref[...])

    o = o_ref[0]
    inv_n = 1.0 / RWKV_HEAD
    mu = _head_sum(o) * inv_n
    od = o - mu
    ovar = _head_sum(od * od) * inv_n
    on = od * lax.rsqrt(ovar + GN_EPS) * lnxg_ref[...] + lnxb_ref[...]
    y_rwkv = _dot(((on + bonus_ref[0]) * g_ref[0]).astype(BF16), rproj_ref[...])

    d = x_ref.shape[2]
    zg = zg_ref[0]
    merged = (zg[:, :d].astype(F32) * y_pool + zg[:, d:2 * d].astype(F32) * y_conv
              + zg[:, 2 * d:].astype(F32) * y_rwkv)
    out_ref[0] = x_ref[0] + mod_ref[0, 2:3, :] * _dot(merged.astype(BF16), wo_ref[...])


def _post(x, mod, zpc, zg, o, bonus, g, poolw, pscale, pproj, convw, convb, lng, lnb, cproj,
          lnxg, lnxb, rproj, wo, ts=512):
    b_, s_, d = x.shape
    const = lambda b, i: (0, 0)
    tile = lambda b, i: (b, i, 0)
    full = lambda a: pl.BlockSpec(a.shape, const)
    in_specs = [pl.BlockSpec((1, ts, d), tile),
                pl.BlockSpec((1, 3, d), lambda b, i: (b, 0, 0)),
                pl.BlockSpec((1, ts, zpc.shape[2]), tile),
                pl.BlockSpec((1, HALO, zpc.shape[2]),
                             lambda b, i: (b, jnp.maximum(i * (ts // HALO) - 1, 0), 0)),
                pl.BlockSpec((1, ts, zg.shape[2]), tile),
                pl.BlockSpec((1, ts, RWKV_W), tile),
                pl.BlockSpec((1, ts, RWKV_W), tile),
                pl.BlockSpec((1, ts, RWKV_W), tile)]
    weights = [poolw, pscale, pproj, convw, convb, lng, lnb, cproj, lnxg, lnxb, rproj, wo]
    in_specs += [full(a) for a in weights]
    return pl.pallas_call(
        _post_kernel,
        grid=(b_, s_ // ts),
        in_specs=in_specs,
        out_specs=pl.BlockSpec((1, ts, d), tile),
        out_shape=jax.ShapeDtypeStruct((b_, s_, d), F32),
        scratch_shapes=[pltpu.VMEM((SUBLANES, HALO + ts, CONV_W), F32)],
        compiler_params=_params(("parallel", "parallel")),
        name="mix_merge",
    )(x, mod, zpc, zpc, zg, o, bonus, g, *weights)


def _router_kernel(x_ref, mod_ref, g_ref, wr_ref, gate_ref, h_ref):
    h = _ada_rms(x_ref[0], mod_ref[0], g_ref[...])
    h_ref[0] = h.astype(BF16)
    logits = jnp.dot(h, wr_ref[...], preferred_element_type=F32, precision=lax.Precision.HIGHEST)
    n_e = logits.shape[1]
    lane = lax.broadcasted_iota(jnp.int32, logits.shape, 1)
    m1 = jnp.max(logits, axis=-1, keepdims=True)
    i1 = jnp.min(jnp.where(logits == m1, lane, n_e), axis=-1, keepdims=True)
    sel1 = lane == i1
    rest = jnp.where(sel1, -jnp.inf, logits)
    m2 = jnp.max(rest, axis=-1, keepdims=True)
    i2 = jnp.min(jnp.where(rest == m2, lane, n_e), axis=-1, keepdims=True)
    sel2 = lane == i2
    e2 = jnp.exp(m2 - m1)
    w1 = 1.0 / (1.0 + e2)
    gate_ref[0] = jnp.where(sel1, w1, 0.0) + jnp.where(sel2, e2 * w1, 0.0)


def _router(x, mod, g, wr, ts=512):
    b_, s_, d = x.shape
    n_e = wr.shape[1]
    tile = lambda b, i: (b, i, 0)
    return pl.pallas_call(
        _router_kernel,
        grid=(b_, s_ // ts),
        in_specs=[pl.BlockSpec((1, ts, d), tile),
                  pl.BlockSpec((1, 3, d), lambda b, i: (b, 0, 0)),
                  pl.BlockSpec((1, d), lambda b, i: (0, 0)),
                  pl.BlockSpec(wr.shape, lambda b, i: (0, 0))],
        out_specs=[pl.BlockSpec((1, ts, n_e), tile), pl.BlockSpec((1, ts, d), tile)],
        out_shape=[jax.ShapeDtypeStruct((b_, s_, n_e), F32), jax.ShapeDtypeStruct((b_, s_, d), BF16)],
        compiler_params=_params(("parallel", "parallel")),
        name="moe_router",
    )(x, mod, g, wr)


PIECE = 128
SUB = 256
SPAN = SUB + PIECE


def _moe_kernel(cnt_ref, first_ref, h_ref, rank_em_ref, rank_tm_ref, gate_ref,
                w1_ref, w3_ref, w2_ref, y_ref, xs_ref, acc_ref):
    b = pl.program_id(0)
    e = pl.program_id(1)
    f = pl.program_id(2)
    n_e = pl.num_programs(1)
    tb = h_ref.shape[0]
    n_sub = tb // SUB
    pair = b * n_e + e
    n_pieces = (cnt_ref[pair] + (PIECE - 1)) // PIECE
    n_big = n_pieces // 2
    has_tail = (n_pieces % 2) == 1

    def span_rows(s):
        return pl.ds(pl.multiple_of(first_ref[pair * n_sub + s] * PIECE, PIECE), SPAN)

    def for_groups(fn):
        def big(i, carry):
            fn(2 * i, 2 * PIECE)
            return carry
        lax.fori_loop(0, n_big, big, 0)

        @pl.when(has_tail)
        def _():
            fn(2 * n_big, PIECE)

    def rows_of(j0, m):
        return pl.ds(pl.multiple_of(j0 * PIECE, PIECE), m)

    @pl.when((e == 0) & (f == 0))
    def _():
        y_ref[...] = jnp.zeros_like(y_ref)

    @pl.when(f == 0)
    def _():
        def clear(j, carry):
            xs_ref[rows_of(j, PIECE), :] = jnp.zeros((PIECE, xs_ref.shape[1]), BF16)
            return carry
        lax.fori_loop(0, jnp.minimum(n_pieces + SPAN // PIECE, tb // PIECE), clear, 0)
        acc_ref[pl.ds(pl.multiple_of(jnp.minimum(n_pieces * PIECE, tb - SPAN), PIECE), SPAN), :] = (
            jnp.zeros((SPAN, acc_ref.shape[1]), F32))
        for s in range(n_sub):
            rows = span_rows(s)
            slot = lax.broadcasted_iota(jnp.int32, (SPAN, 1), 0) + first_ref[pair * n_sub + s] * PIECE
            onehot = jnp.where(rank_em_ref[0, 0, s] == slot, 1.0, 0.0).astype(BF16)
            xs_ref[rows, :] += _dot(onehot, h_ref[s * SUB:(s + 1) * SUB, :]).astype(BF16)

    def expert(j0, m):
        rows = rows_of(j0, m)
        xj = xs_ref[rows, :]
        act = _silu(_dot(xj, w1_ref[0])) * _dot(xj, w3_ref[0])
        part = _dot(act.astype(BF16), w2_ref[0])

        @pl.when(f == 0)
        def _():
            acc_ref[rows, :] = part

        @pl.when(f > 0)
        def _():
            acc_ref[rows, :] += part
    for_groups(expert)

    @pl.when(f == pl.num_programs(2) - 1)
    def _():
        lane_e = lax.broadcasted_iota(jnp.int32, (1, n_e), 1) == e
        for s in range(n_sub):
            tok = slice(s * SUB, (s + 1) * SUB)
            slot = lax.broadcasted_iota(jnp.int32, (1, SPAN), 1) + first_ref[pair * n_sub + s] * PIECE
            rank_col = jnp.sum(jnp.where(lane_e, rank_tm_ref[tok, :], 0), axis=-1, keepdims=True)
            gate_col = jnp.sum(jnp.where(lane_e, gate_ref[tok, :], 0.0), axis=-1, keepdims=True)
            onehot_t = jnp.where(rank_col == slot, 1.0, 0.0).astype(BF16)
            y_ref[tok, :] += gate_col * _dot(onehot_t, acc_ref[span_rows(s), :].astype(BF16))


def _moe(h, gate, w1, w3, w2, tb=2048, tf=1792):
    t_, d = h.shape
    n_e, _, ff = w1.shape
    nb = t_ // tb
    n_sub = tb // SUB
    sel = (gate != 0.0).reshape(nb, tb, n_e).astype(jnp.int32)
    rank_tm = jnp.where(sel > 0, jnp.cumsum(sel, axis=1) - 1, -1)
    ends = jnp.cumsum(jnp.sum(sel.reshape(nb, n_sub, SUB, n_e), axis=2), axis=1)
    ends = jnp.transpose(ends, (0, 2, 1))
    cnt = ends[:, :, -1]
    starts = jnp.concatenate([jnp.zeros_like(ends[:, :, :1]), ends[:, :, :-1]], axis=2)
    first = jnp.minimum(starts // PIECE, (tb - SPAN) // PIECE)
    rank_em = jnp.transpose(rank_tm, (0, 2, 1)).reshape(nb, n_e, n_sub, 1, SUB)
    once = pl.Buffered(1)
    grid_spec = pltpu.PrefetchScalarGridSpec(
        num_scalar_prefetch=2,
        grid=(nb, n_e, ff // tf),
        in_specs=[pl.BlockSpec((tb, d), lambda b, e, f, *_: (b, 0), pipeline_mode=once),
                  pl.BlockSpec((1, 1, n_sub, 1, SUB), lambda b, e, f, *_: (b, e, 0, 0, 0)),
                  pl.BlockSpec((tb, n_e), lambda b, e, f, *_: (b, 0), pipeline_mode=once),
                  pl.BlockSpec((tb, n_e), lambda b, e, f, *_: (b, 0), pipeline_mode=once),
                  pl.BlockSpec((1, d, tf), lambda b, e, f, *_: (e, 0, f)),
                  pl.BlockSpec((1, d, tf), lambda b, e, f, *_: (e, 0, f)),
                  pl.BlockSpec((1, tf, d), lambda b, e, f, *_: (e, f, 0))],
        out_specs=pl.BlockSpec((tb, d), lambda b, e, f, *_: (b, 0), pipeline_mode=once),
        scratch_shapes=[pltpu.VMEM((tb, d), BF16), pltpu.VMEM((tb, d), F32)])
    return pl.pallas_call(
        _moe_kernel,
        grid_spec=grid_spec,
        out_shape=jax.ShapeDtypeStruct((t_, d), F32),
        compiler_params=_params(("parallel", "arbitrary", "arbitrary")),
        name="moe_sparse",
    )(cnt.reshape(-1), first.reshape(-1), h, rank_em, rank_tm.reshape(t_, n_e), gate, w1, w3, w2)


def _ffn_kernel(x_ref, mod_ref, g_ref, w1_ref, w3_ref, w2_ref, out_ref):
    x = x_ref[0]
    h = _ada_rms(x, mod_ref[0], g_ref[...]).astype(BF16)
    act = _silu(_dot(h, w1_ref[...])) * _dot(h, w3_ref[...])
    out_ref[0] = x + mod_ref[0, 2:3, :] * _dot(act.astype(BF16), w2_ref[...])


def _ffn(x, mod, g, w1, w3, w2, tm=256):
    b_, s_, d = x.shape
    tile = lambda b, i: (b, i, 0)
    const = lambda b, i: (0, 0)
    once = pl.Buffered(1)
    return pl.pallas_call(
        _ffn_kernel,
        grid=(b_, s_ // tm),
        in_specs=[pl.BlockSpec((1, tm, d), tile),
                  pl.BlockSpec((1, 3, d), lambda b, i: (b, 0, 0)),
                  pl.BlockSpec((1, d), const),
                  pl.BlockSpec(w1.shape, const, pipeline_mode=once),
                  pl.BlockSpec(w3.shape, const, pipeline_mode=once),
                  pl.BlockSpec(w2.shape, const, pipeline_mode=once)],
        out_specs=pl.BlockSpec((1, tm, d), tile),
        out_shape=jax.ShapeDtypeStruct((b_, s_, d), F32),
        compiler_params=_params(("parallel", "parallel")),
        name="ffn",
    )(x, mod, g, w1, w3, w2)


def _finish_kernel(has_y, has_norm, x_ref, *rest):
    x = x_ref[0]
    if has_y:
        y_ref, mod_ref = rest[:2]
        rest = rest[2:]
        x = x + mod_ref[0, 2:3, :] * y_ref[0]
    if has_norm:
        g_ref = rest[0]
        rest = rest[1:]
        x = x * lax.rsqrt(jnp.mean(x * x, axis=-1, keepdims=True) + RMS_EPS) * g_ref[...]
    rest[0][0] = x


def _finish(x, y=None, mod=None, g=None, ts=512):
    b_, s_, d = x.shape
    tile = lambda b, i: (b, i, 0)
    in_specs = [pl.BlockSpec((1, ts, d), tile)]
    args = [x]
    if y is not None:
        in_specs += [pl.BlockSpec((1, ts, d), tile), pl.BlockSpec((1, 3, d), lambda b, i: (b, 0, 0))]
        args += [y, mod]
    if g is not None:
        in_specs.append(pl.BlockSpec((1, d), lambda b, i: (0, 0)))
        args.append(g)
    return pl.pallas_call(
        functools.partial(_finish_kernel, y is not None, g is not None),
        grid=(b_, s_ // ts),
        in_specs=in_specs,
        out_specs=pl.BlockSpec((1, ts, d), tile),
        out_shape=jax.ShapeDtypeStruct((b_, s_, d), F32),
        compiler_params=_params(("parallel", "parallel")),
        name="finish",
    )(*args)


def _block_diag(w):
    g_, c_, _ = w.shape
    out = jnp.zeros((g_ * c_, g_ * c_), w.dtype)
    for gi in range(g_):
        out = out.at[gi * c_:(gi + 1) * c_, gi * c_:(gi + 1) * c_].set(w[gi])
    return out


def kernel(x, c, ada_w, ada_b, norm_g, w_in, pool_w, pool_scale, pool_proj, conv_w, conv_b, conv_ln_g, conv_ln_b, conv_proj, rwkv_mu, rwkv_w0, rwkv_w2, rwkv_a0, rwkv_a2, rwkv_g2, rwkv_kk_scale, rwkv_ka, rwkv_rk, rwkv_lnx_g, rwkv_lnx_b, rwkv_proj, vres_w_down, vres_mu, vres_v0, vres_v2, w_o, ffn_w1, ffn_w3, ffn_w2, moe_router, moe_w1, moe_w3, moe_w2, final_norm_g):
    depth = w_in.shape[0]
    d = x.shape[2]
    row = lambda a: a.reshape(1, -1)
    mods = _ada_mods(c, ada_w, ada_b)
    v_first = None
    for l in range(depth):
        mod = mods[2 * l]
        zr_pad = jnp.zeros((d, ZR_COLS - RWKV_COLS - V_LORA), F32)
        mu_pad = jnp.zeros((ZR_COLS - RWKV_COLS - V_LORA,), F32)
        if l == 0:
            w_vd = jnp.zeros((d, V_LORA), F32)
            mu_vd = jnp.zeros((V_LORA,), F32)
        else:
            w_vd = vres_w_down[l - 1]
            mu_vd = vres_mu[l - 1]
        wr = jnp.concatenate([w_in[l][:, OFF_RWKV:OFF_GATE], w_vd, zr_pad], axis=1).astype(BF16)
        mu = jnp.concatenate([rwkv_mu[l], mu_vd, mu_pad]).reshape(1, ZR_COLS)
        wpc = w_in[l][:, :OFF_RWKV].astype(BF16)
        wg = w_in[l][:, OFF_GATE:OFF_GATE + 3 * d].astype(BF16)
        vres = None
        if l > 0:
            vres = (v_first, row(vres_v0[l - 1]), vres_v2[l - 1].astype(BF16))
        zpc, zg, r, lw, kh, v, kk, bb, g, bonus = _in_proj(
            x, mod, row(norm_g[l, 0]), wpc, wr, wg,
            mu, row(rwkv_w0[l]), rwkv_w2[l].astype(BF16), row(rwkv_a0[l]),
            rwkv_a2[l].astype(BF16), rwkv_g2[l].astype(BF16), row(rwkv_kk_scale[l]),
            row(rwkv_ka[l]), row(rwkv_rk[l]), vres)
        if l == 0:
            v_first = v
        o = _wkv_scan(r, lw, kh, v, kk, bb)
        x = _post(x, mod, zpc, zg, o, bonus, g,
                  _block_diag(pool_w[l]).astype(BF16), row(pool_scale[l]), pool_proj[l].astype(BF16),
                  conv_w[l], row(conv_b[l]), row(conv_ln_g[l]), row(conv_ln_b[l]),
                  conv_proj[l].astype(BF16), row(rwkv_lnx_g[l]), row(rwkv_lnx_b[l]),
                  rwkv_proj[l].astype(BF16), w_o[l].astype(BF16))
        mod = mods[2 * l + 1]
        gn = row(norm_g[l, 1])
        i = l // 2
        last = l == depth - 1
        if l % 2 == 0:
            x = _ffn(x, mod, gn, ffn_w1[i].astype(BF16), ffn_w3[i].astype(BF16), ffn_w2[i].astype(BF16))
            if last:
                x = _finish(x, g=row(final_norm_g))
        else:
            gate, h = _router(x, mod, gn, moe_router[i])
            n_e = gate.shape[2]
            y = _moe(h.reshape(-1, d), gate.reshape(-1, n_e), moe_w1[i].astype(BF16),
                     moe_w3[i].astype(BF16), moe_w2[i].astype(BF16)).reshape(x.shape)
            x = _finish(x, y, mod, row(final_norm_g) if last else None)
    return x
```
